```python
import jax, jax.numpy as jnp
from jax import lax
import numpy as np

D_MODEL = 1024
BATCH = 8
SEQ = 2048
DEPTH = 2
DEC_BATCH = 128
DEC_SEQ = 1
PAST_LEN = 16384
PAGE_SIZE = 128

CHUNK = 128
A_GROUPS = 4
A_WIDTH = D_MODEL // 4
A_KERNEL = 3
B_HEADS = 4
B_HEAD_DIM = D_MODEL // 8
B_WIDTH = B_HEADS * B_HEAD_DIM
C_GROUPS = 4
C_WIDTH = D_MODEL // 4
C_KERNEL = 31
D_MIX = A_WIDTH + B_WIDTH + C_WIDTH
D_IN = 3 * A_WIDTH + 2 * B_WIDTH + 2 * C_WIDTH
SPLITS = [A_WIDTH, 2 * A_WIDTH, 3 * A_WIDTH,
          3 * A_WIDTH + B_WIDTH, 3 * A_WIDTH + 2 * B_WIDTH,
          3 * A_WIDTH + 2 * B_WIDTH + C_WIDTH]
D_FF = 256 * ((8 * D_MODEL // 3 + 255) // 256)
PLE_DIM = 256
EPS = 1e-6

kernel_name = "hybrid_conv_gmlp_conformer_decode_step"


def rms_norm(x, g):
    xf = x.astype(jnp.float32)
    y = xf * lax.rsqrt(jnp.mean(xf * xf, axis=-1, keepdims=True) + EPS)
    return (y * g.astype(jnp.float32)).astype(x.dtype)


def layer_norm(x, g, b):
    xf = x.astype(jnp.float32)
    mu = jnp.mean(xf, axis=-1, keepdims=True)
    xc = xf - mu
    var = jnp.mean(xc * xc, axis=-1, keepdims=True)
    y = xc * lax.rsqrt(var + EPS) * g.astype(jnp.float32) + b.astype(jnp.float32)
    return y.astype(x.dtype)


def swiglu(h, wg, wu, wd):
    return (jax.nn.silu(h @ wg) * (h @ wu)) @ wd


def depthwise_causal_conv(full, w):
    c = w.shape[1]
    return lax.conv_general_dilated(full, w[:, None, :], window_strides=(1,), padding='VALID',
                                    dimension_numbers=('NWC', 'WIO', 'NWC'),
                                    feature_group_count=c)


def chunk_spatial_mix(v, ws, bias):
    n, t, h, d = v.shape
    L = min(t, CHUNK)
    mask = jnp.tril(jnp.ones((L, L), dtype=bool))
    w = jnp.where(mask[None], ws[:, :L, :L], 0)
    vc = v.reshape(n, t // L, L, h, d)
    out = jnp.einsum('hij,ncjhd->ncihd', w, vc)
    out = out + jnp.swapaxes(bias[:, :L], 0, 1)[None, None, :, :, None]
    return out.reshape(n, t, h, d)


def trunk_layer(x, p, hist_a, hist_c, lw):
    (f1_pre, f1_post, f1_wg, f1_wu, f1_wd, m_pre, m_post, w_in, w_out, a_cw,
     b_lng, b_lnb, b_ws, b_bias, c_cw, c_cb, c_lng, c_lnb,
     f2_pre, f2_post, f2_wg, f2_wu, f2_wd, e_pre, e_post, e_wg, e_wp) = lw
    n, t, _ = x.shape
    x = x + 0.5 * rms_norm(swiglu(rms_norm(x, f1_pre), f1_wg, f1_wu, f1_wd), f1_post)
    h = rms_norm(x, m_pre)
    z = h @ w_in
    a_val, a_c, a_b, b_u, b_v, c_val, c_gate = jnp.split(z, SPLITS, axis=-1)
    fa = jnp.concatenate([hist_a, a_c * a_val], axis=1)
    y_a = a_b * depthwise_causal_conv(fa, a_cw)
    new_a = fa[:, -(A_KERNEL - 1):]
    u = jax.nn.gelu(b_u)
    v = layer_norm(jax.nn.gelu(b_v).reshape(n, t, B_HEADS, B_HEAD_DIM), b_lng, b_lnb)
    y_b = u * chunk_spatial_mix(v, b_ws, b_bias).reshape(n, t, B_WIDTH)
    fc = jnp.concatenate([hist_c, c_val * jax.nn.sigmoid(c_gate)], axis=1)
    y_c = jax.nn.silu(layer_norm(depthwise_causal_conv(fc, c_cw) + c_cb, c_lng, c_lnb))
    new_c = fc[:, -(C_KERNEL - 1):]
    mix = jnp.concatenate([y_a, y_b, y_c], axis=-1) @ w_out
    x = x + rms_norm(mix, m_post)
    x = x + 0.5 * rms_norm(swiglu(rms_norm(x, f2_pre), f2_wg, f2_wu, f2_wd), f2_post)
    gate = jax.nn.sigmoid(rms_norm(x, e_pre) @ e_wg)
    x = x + rms_norm(gate * (p @ e_wp), e_post)
    return x, new_a, new_c, v.reshape(n, t, B_WIDTH)


def setup_inputs(seed: int = 0) -> dict:
    key = jax.random.key(seed)
    keys = iter(jax.random.split(key, 48))

    def nrm(shape, scale):
        return jax.random.normal(next(keys), shape, jnp.float32) * scale

    def gain(shape):
        return 1.0 + nrm(shape, 0.05)

    L = DEPTH
    return {
        "x_prompt": nrm((BATCH, SEQ, D_MODEL), 1.0),
        "x_sample": nrm((DEC_BATCH, DEC_SEQ, D_MODEL), 1.0),
        "state_conv_a": nrm((L, DEC_BATCH, A_KERNEL - 1, A_WIDTH), 1.0),
        "state_conv_c": nrm((L, DEC_BATCH, C_KERNEL - 1, C_WIDTH), 1.0),
        "p_prompt": nrm((L, BATCH, SEQ, PLE_DIM), 1.0),
        "p_sample": nrm((L, DEC_BATCH, DEC_SEQ, PLE_DIM), 1.0),
        "f1_pre": gain((L, D_MODEL)),
        "f1_post": gain((L, D_MODEL)),
        "f1_wg": nrm((L, D_MODEL, D_FF), D_MODEL ** -0.5),
        "f1_wu": nrm((L, D_MODEL, D_FF), D_MODEL ** -0.5),
        "f1_wd": nrm((L, D_FF, D_MODEL), D_FF ** -0.5),
        "m_pre": gain((L, D_MODEL)),
        "m_post": gain((L, D_MODEL)),
        "w_in": nrm((L, D_MODEL, D_IN), D_MODEL ** -0.5),
        "w_out": nrm((L, D_MIX, D_MODEL), D_MIX ** -0.5),
        "a_conv_w": nrm((L, A_KERNEL, A_WIDTH), A_KERNEL ** -0.5),
        "b_ln_g": gain((L, B_HEADS, B_HEAD_DIM)),
        "b_ln_b": nrm((L, B_HEADS, B_HEAD_DIM), 0.02),
        "b_ws": nrm((L, B_HEADS, CHUNK, CHUNK), CHUNK ** -0.5),
        "b_bias": 1.0 + nrm((L, B_HEADS, CHUNK), 0.02),
        "c_conv_w": nrm((L, C_KERNEL, C_WIDTH), C_KERNEL ** -0.5),
        "c_conv_b": nrm((L, C_WIDTH), 0.02),
        "c_ln_g": gain((L, C_WIDTH)),
        "c_ln_b": nrm((L, C_WIDTH), 0.02),
        "f2_pre": gain((L, D_MODEL)),
        "f2_post": gain((L, D_MODEL)),
        "f2_wg": nrm((L, D_MODEL, D_FF), D_MODEL ** -0.5),
        "f2_wu": nrm((L, D_MODEL, D_FF), D_MODEL ** -0.5),
        "f2_wd": nrm((L, D_FF, D_MODEL), D_FF ** -0.5),
        "e_pre": gain((L, D_MODEL)),
        "e_post": gain((L, D_MODEL)),
        "e_wg": nrm((L, D_MODEL, D_MODEL), D_MODEL ** -0.5),
        "e_wp": nrm((L, PLE_DIM, D_MODEL), PLE_DIM ** -0.5),
    }


def reference(x_prompt, x_sample, state_conv_a, state_conv_c, p_prompt, p_sample,
              f1_pre, f1_post, f1_wg, f1_wu, f1_wd, m_pre, m_post, w_in, w_out, a_conv_w,
              b_ln_g, b_ln_b, b_ws, b_bias, c_conv_w, c_conv_b, c_ln_g, c_ln_b,
              f2_pre, f2_post, f2_wg, f2_wu, f2_wd, e_pre, e_post, e_wg, e_wp):
    zeros_a = jnp.zeros((x_prompt.shape[0], A_KERNEL - 1, A_WIDTH), x_prompt.dtype)
    zeros_c = jnp.zeros((x_prompt.shape[0], C_KERNEL - 1, C_WIDTH), x_prompt.dtype)
    yp, ys = x_prompt, x_sample
    a_p, c_p, a_s, c_s, v_s = [], [], [], [], []
    for i in range(DEPTH):
        lw = tuple(w[i] for w in (f1_pre, f1_post, f1_wg, f1_wu, f1_wd, m_pre, m_post, w_in,
                                  w_out, a_conv_w, b_ln_g, b_ln_b, b_ws, b_bias, c_conv_w,
                                  c_conv_b, c_ln_g, c_ln_b, f2_pre, f2_post, f2_wg, f2_wu,
                                  f2_wd, e_pre, e_post, e_wg, e_wp))
        yp, na_p, nc_p, _ = trunk_layer(yp, p_prompt[i], zeros_a, zeros_c, lw)
        ys, na_s, nc_s, vr_s = trunk_layer(ys, p_sample[i], state_conv_a[i], state_conv_c[i], lw)
        a_p.append(na_p)
        c_p.append(nc_p)
        a_s.append(na_s)
        c_s.append(nc_s)
        v_s.append(vr_s)
    return (yp, ys, jnp.stack(a_p), jnp.stack(c_p), jnp.stack(a_s), jnp.stack(c_s), jnp.stack(v_s))
```

```python
import jax
import jax.numpy as jnp
from jax import lax
from jax.experimental import pallas as pl
from jax.experimental.pallas import tpu as pltpu

EPS = 1e-6
CHUNK = 128
A_KERNEL = 3
C_KERNEL = 31
B_HEADS = 4

V7X_SUBLANES = 8
V7X_LANES = 128
V7X_MXU_COLS = 256
V7X_VMEM_BYTES = 64 * 1024 * 1024

PROMPT_TILE = 512
FF_COLS = V7X_MXU_COLS
CONV_ROWS = 32
A_HIST_PAD = V7X_SUBLANES
C_HIST_PAD = 4 * V7X_SUBLANES

F32 = jnp.float32
BF16 = jnp.bfloat16


def _vmem_limit(block_bytes):
    need = int(block_bytes * 1.25) + (8 << 20)
    return min(need, V7X_VMEM_BYTES - (4 << 20))


def _rms(x, g):
    return x * lax.rsqrt(jnp.mean(x * x, axis=-1, keepdims=True) + EPS) * g


def _ln(x, g, b):
    mu = jnp.mean(x, axis=-1, keepdims=True)
    xc = x - mu
    var = jnp.mean(xc * xc, axis=-1, keepdims=True)
    return xc * lax.rsqrt(var + EPS) * g + b


def _dot(a, b):
    return jnp.dot(a, b, preferred_element_type=F32)


def _resident(shape):
    zeros = (0,) * len(shape)
    return pl.BlockSpec(shape, lambda *_: zeros, pipeline_mode=pl.Buffered(1))


def _nbytes(shape, dtype):
    n = 1
    for s in shape:
        n *= s
    return n * jnp.dtype(dtype).itemsize


def _ffn_body(x_ref, pre_ref, post_ref, wg_ref, wu_ref, wd_ref, o_ref, act_ref):
    x = x_ref[...]
    h = _rms(x, pre_ref[...]).astype(BF16)
    d_ff = wg_ref.shape[1]
    for c0 in range(0, d_ff, FF_COLS):
        g = _dot(h, wg_ref[:, c0:c0 + FF_COLS])
        u = _dot(h, wu_ref[:, c0:c0 + FF_COLS])
        act_ref[:, c0:c0 + FF_COLS] = (jax.nn.silu(g) * u).astype(BF16)
    y = _dot(act_ref[...], wd_ref[...])
    o_ref[...] = x + 0.5 * _rms(y, post_ref[...])


def _ffn(x, pre, post, wg, wu, wd, tile):
    n, d = x.shape
    d_ff = wg.shape[1]
    assert n % tile == 0 and d_ff % FF_COLS == 0
    row = pl.BlockSpec((tile, d), lambda i: (i, 0))
    blocks = (4 * _nbytes((tile, d), F32) + 3 * _nbytes((d, d_ff), BF16)
              + _nbytes((tile, d_ff), BF16))
    return pl.pallas_call(
        _ffn_body,
        grid=(n // tile,),
        in_specs=[row, _resident((1, d)), _resident((1, d)),
                  _resident((d, d_ff)), _resident((d, d_ff)), _resident((d_ff, d))],
        out_specs=row,
        out_shape=jax.ShapeDtypeStruct((n, d), F32),
        scratch_shapes=[pltpu.VMEM((tile, d_ff), BF16)],
        compiler_params=pltpu.CompilerParams(
            dimension_semantics=("arbitrary",), vmem_limit_bytes=_vmem_limit(blocks)),
        name="ffn",
    )(x, pre, post, wg, wu, wd)


def _embed_body(x_ref, p_ref, pre_ref, post_ref, wg_ref, wp_ref, o_ref):
    x = x_ref[...]
    h = _rms(x, pre_ref[...]).astype(BF16)
    gate = jax.nn.sigmoid(_dot(h, wg_ref[...]))
    pe = _dot(p_ref[...].astype(BF16), wp_ref[...])
    o_ref[...] = x + _rms(gate * pe, post_ref[...])


def _embed(x, p, pre, post, wg, wp, tile):
    n, d = x.shape
    dp = p.shape[1]
    assert n % tile == 0
    row = pl.BlockSpec((tile, d), lambda i: (i, 0))
    blocks = (4 * _nbytes((tile, d), F32) + 2 * _nbytes((tile, dp), F32)
              + _nbytes((d, d), BF16) + _nbytes((dp, d), BF16))
    return pl.pallas_call(
        _embed_body,
        grid=(n // tile,),
        in_specs=[row, pl.BlockSpec((tile, dp), lambda i: (i, 0)),
                  _resident((1, d)), _resident((1, d)), _resident((d, d)), _resident((dp, d))],
        out_specs=row,
        out_shape=jax.ShapeDtypeStruct((n, d), F32),
        compiler_params=pltpu.CompilerParams(
            dimension_semantics=("arbitrary",), vmem_limit_bytes=_vmem_limit(blocks)),
        name="embed",
    )(x, p, pre, post, wg, wp)


def _mixer_prompt_body(x_ref, pre_ref, post_ref, win_ref, wout_ref, acw_ref, lng_ref, lnb_ref,
                       ws_ref, bias_ref, ccw_ref, ccb_ref, clg_ref, clb_ref,
                       o_ref, na_ref, nc_ref, fa_ref, fc_ref, mix_ref):
    tt = x_ref.shape[1]
    aw = acw_ref.shape[1]
    bw = ws_ref.shape[0] * ws_ref.shape[1]
    hd = ws_ref.shape[1]
    cw = ccw_ref.shape[1]
    t = pl.program_id(1)

    @pl.when(t == 0)
    def _():
        fa_ref[0:A_HIST_PAD, :] = jnp.zeros((A_HIST_PAD, aw), F32)
        fc_ref[0:C_HIST_PAD, :] = jnp.zeros((C_HIST_PAD, cw), F32)

    x = x_ref[0]
    h = _rms(x, pre_ref[...]).astype(BF16)

    o_a = 0
    za = _dot(h, win_ref[:, o_a:o_a + 3 * aw])
    fa_ref[A_HIST_PAD:A_HIST_PAD + tt, :] = za[:, aw:2 * aw] * za[:, 0:aw]
    a_b = za[:, 2 * aw:3 * aw]
    base = A_HIST_PAD - (A_KERNEL - 1)
    conv_a = acw_ref[0:1, :] * fa_ref[base:base + tt, :]
    for k in range(1, A_KERNEL):
        conv_a = conv_a + acw_ref[k:k + 1, :] * fa_ref[base + k:base + k + tt, :]
    mix_ref[:, 0:aw] = (a_b * conv_a).astype(BF16)
    last_a = fa_ref[base + tt:base + tt + A_KERNEL - 1, :]
    na_ref[0] = last_a
    fa_ref[base:A_HIST_PAD, :] = last_a

    o_b = 3 * aw
    u = jax.nn.gelu(_dot(h, win_ref[:, o_b:o_b + bw]))
    gv = jax.nn.gelu(_dot(h, win_ref[:, o_b + bw:o_b + 2 * bw]))
    row_i = lax.broadcasted_iota(jnp.int32, (CHUNK, CHUNK), 0)
    col_j = lax.broadcasted_iota(jnp.int32, (CHUNK, CHUNK), 1)
    for hh in range(B_HEADS):
        cols = slice(hh * hd, (hh + 1) * hd)
        v = _ln(gv[:, cols], lng_ref[hh:hh + 1, :], lnb_ref[hh:hh + 1, :]).astype(BF16)
        w_tril = jnp.where(col_j <= row_i, ws_ref[hh], 0.0).astype(BF16)
        bias = bias_ref[:, hh:hh + 1]
        for c0 in range(0, tt, CHUNK):
            sp = _dot(w_tril, v[c0:c0 + CHUNK, :]) + bias
            mix_ref[c0:c0 + CHUNK, aw + hh * hd:aw + (hh + 1) * hd] = (
                u[c0:c0 + CHUNK, cols] * sp).astype(BF16)

    o_c = o_b + 2 * bw
    zc = _dot(h, win_ref[:, o_c:o_c + 2 * cw])
    fc_ref[C_HIST_PAD:C_HIST_PAD + tt, :] = zc[:, 0:cw] * jax.nn.sigmoid(zc[:, cw:2 * cw])
    base = C_HIST_PAD - (C_KERNEL - 1)
    for r0 in range(0, tt, CONV_ROWS):
        acc = ccw_ref[0:1, :] * fc_ref[base + r0:base + r0 + CONV_ROWS, :] + ccb_ref[...]
        for k in range(1, C_KERNEL):
            acc = acc + ccw_ref[k:k + 1, :] * fc_ref[base + r0 + k:base + r0 + k + CONV_ROWS, :]
        y_c = jax.nn.silu(_ln(acc, clg_ref[...], clb_ref[...]))
        mix_ref[r0:r0 + CONV_ROWS, aw + bw:aw + bw + cw] = y_c.astype(BF16)
    last_c = fc_ref[base + tt:base + tt + C_KERNEL - 1, :]
    nc_ref[0] = last_c
    fc_ref[base:C_HIST_PAD, :] = last_c

    mix = _dot(mix_ref[...], wout_ref[...])
    o_ref[0] = x + _rms(mix, post_ref[...])


def _mixer_prompt(x, pre, post, w_in, w_out, a_cw, lng, lnb, ws, bias_t, c_cw, c_cb, clg, clb, tile):
    b, s, d = x.shape
    d_in = w_in.shape[1]
    d_mix = w_out.shape[0]
    aw, cw = a_cw.shape[1], c_cw.shape[1]
    assert s % tile == 0 and tile % CHUNK == 0 and tile % CONV_ROWS == 0
    assert ws.shape[1] == CHUNK and ws.shape[2] == CHUNK
    xblk = pl.BlockSpec((1, tile, d), lambda i, t: (i, t, 0))
    small = [pre, post]
    params = [a_cw, lng, lnb, ws, bias_t, c_cw, c_cb, clg, clb]
    blocks = (4 * _nbytes((tile, d), F32) + _nbytes((d, d_in), BF16) + _nbytes((d_mix, d), BF16)
              + _nbytes((tile, d_mix), BF16) + 2 * _nbytes((tile, d_in), F32))
    return pl.pallas_call(
        _mixer_prompt_body,
        grid=(b, s // tile),
        in_specs=[xblk] + [_resident(a.shape) for a in small]
                 + [_resident(w_in.shape), _resident(w_out.shape)]
                 + [_resident(a.shape) for a in params],
        out_specs=[xblk,
                   pl.BlockSpec((1, A_KERNEL - 1, aw), lambda i, t: (i, 0, 0)),
                   pl.BlockSpec((1, C_KERNEL - 1, cw), lambda i, t: (i, 0, 0))],
        out_shape=[jax.ShapeDtypeStruct((b, s, d), F32),
                   jax.ShapeDtypeStruct((b, A_KERNEL - 1, aw), F32),
                   jax.ShapeDtypeStruct((b, C_KERNEL - 1, cw), F32)],
        scratch_shapes=[pltpu.VMEM((A_HIST_PAD + tile, aw), F32),
                        pltpu.VMEM((C_HIST_PAD + tile, cw), F32),
                        pltpu.VMEM((tile, d_mix), BF16)],
        compiler_params=pltpu.CompilerParams(
            dimension_semantics=("arbitrary", "arbitrary"),
            vmem_limit_bytes=_vmem_limit(blocks)),
        name="mixer_prompt",
    )(x, *small, w_in, w_out, *params)


def _mixer_sample_body(x_ref, sa_ref, sc_ref, pre_ref, post_ref, win_ref, wout_ref, acw_ref,
                       lng_ref, lnb_ref, w00_ref, b0_ref, ccw_ref, ccb_ref, clg_ref, clb_ref,
                       o_ref, na_ref, nc_ref, v_ref, mix_ref):
    aw = acw_ref.shape[1]
    bw = w00_ref.shape[1]
    hd = bw // B_HEADS
    cw = ccw_ref.shape[1]
    x = x_ref[...]
    h = _rms(x, pre_ref[...]).astype(BF16)

    za = _dot(h, win_ref[:, 0:3 * aw])
    fa_new = za[:, aw:2 * aw] * za[:, 0:aw]
    conv_a = acw_ref[A_KERNEL - 1:A_KERNEL, :] * fa_new
    for k in range(A_KERNEL - 1):
        conv_a = conv_a + acw_ref[k:k + 1, :] * sa_ref[:, k * aw:(k + 1) * aw]
    mix_ref[:, 0:aw] = (za[:, 2 * aw:3 * aw] * conv_a).astype(BF16)
    na_ref[:, 0:(A_KERNEL - 2) * aw] = sa_ref[:, aw:(A_KERNEL - 1) * aw]
    na_ref[:, (A_KERNEL - 2) * aw:(A_KERNEL - 1) * aw] = fa_new

    o_b = 3 * aw
    u = jax.nn.gelu(_dot(h, win_ref[:, o_b:o_b + bw]))
    gv = jax.nn.gelu(_dot(h, win_ref[:, o_b + bw:o_b + 2 * bw]))
    for hh in range(B_HEADS):
        cols = slice(hh * hd, (hh + 1) * hd)
        v = _ln(gv[:, cols], lng_ref[hh:hh + 1, :], lnb_ref[hh:hh + 1, :])
        v_ref[:, cols] = v
        sp = w00_ref[:, cols] * v + b0_ref[:, cols]
        mix_ref[:, aw + hh * hd:aw + (hh + 1) * hd] = (u[:, cols] * sp).astype(BF16)

    o_c = o_b + 2 * bw
    zc = _dot(h, win_ref[:, o_c:o_c + 2 * cw])
    fc_new = zc[:, 0:cw] * jax.nn.sigmoid(zc[:, cw:2 * cw])
    acc = ccw_ref[C_KERNEL - 1:C_KERNEL, :] * fc_new + ccb_ref[...]
    for k in range(C_KERNEL - 1):
        acc = acc + ccw_ref[k:k + 1, :] * sc_ref[:, k * cw:(k + 1) * cw]
    y_c = jax.nn.silu(_ln(acc, clg_ref[...], clb_ref[...]))
    mix_ref[:, aw + bw:aw + bw + cw] = y_c.astype(BF16)
    nc_ref[:, 0:(C_KERNEL - 2) * cw] = sc_ref[:, cw:(C_KERNEL - 1) * cw]
    nc_ref[:, (C_KERNEL - 2) * cw:(C_KERNEL - 1) * cw] = fc_new

    mix = _dot(mix_ref[...], wout_ref[...])
    o_ref[...] = x + _rms(mix, post_ref[...])


def _mixer_sample(x, sa, sc, pre, post, w_in, w_out, a_cw, lng, lnb, w00, b0, c_cw, c_cb, clg, clb):
    n, d = x.shape
    d_mix = w_out.shape[0]
    bw = w00.shape[1]
    args = [x, sa, sc, pre, post, w_in, w_out, a_cw, lng, lnb, w00, b0, c_cw, c_cb, clg, clb]
    blocks = 2 * sum(_nbytes(a.shape, a.dtype) for a in args) + 2 * _nbytes(sc.shape, F32)
    return pl.pallas_call(
        _mixer_sample_body,
        out_shape=[jax.ShapeDtypeStruct(x.shape, F32), jax.ShapeDtypeStruct(sa.shape, F32),
                   jax.ShapeDtypeStruct(sc.shape, F32), jax.ShapeDtypeStruct((n, bw), F32)],
        scratch_shapes=[pltpu.VMEM((n, d_mix), BF16)],
        compiler_params=pltpu.CompilerParams(vmem_limit_bytes=_vmem_limit(blocks)),
        name="mixer_sample",
    )(*args)


def kernel(x_prompt, x_sample, state_conv_a, state_conv_c, p_prompt, p_sample, f1_pre, f1_post, f1_wg, f1_wu, f1_wd, m_pre, m_post, w_in, w_out, a_conv_w, b_ln_g, b_ln_b, b_ws, b_bias, c_conv_w, c_conv_b, c_ln_g, c_ln_b, f2_pre, f2_post, f2_wg, f2_wu, f2_wd, e_pre, e_post, e_wg, e_wp):
    depth = f1_wg.shape[0]
    nb, seq, d = x_prompt.shape
    ns = x_sample.shape[0]
    assert x_sample.shape[1] == 1
    aw, cw = a_conv_w.shape[2], c_conv_w.shape[2]
    hd = b_ws.shape[2]
    bw = B_HEADS * hd

    yp = x_prompt
    ys = x_sample.reshape(ns, d)
    a_p, c_p, a_s, c_s, v_s = [], [], [], [], []
    for i in range(depth):
        row = lambda g: g[i].reshape(1, -1)
        f1 = (row(f1_pre), row(f1_post), f1_wg[i].astype(BF16), f1_wu[i].astype(BF16),
              f1_wd[i].astype(BF16))
        f2 = (row(f2_pre), row(f2_post), f2_wg[i].astype(BF16), f2_wu[i].astype(BF16),
              f2_wd[i].astype(BF16))
        em = (row(e_pre), row(e_post), e_wg[i].astype(BF16), e_wp[i].astype(BF16))
        mx_w = (row(m_pre), row(m_post), w_in[i].astype(BF16), w_out[i].astype(BF16))
        conv_ln = (c_conv_w[i], row(c_conv_b), row(c_ln_g), row(c_ln_b))

        yp = _ffn(yp.reshape(nb * seq, d), *f1, tile=PROMPT_TILE).reshape(nb, seq, d)
        yp, na_p, nc_p = _mixer_prompt(
            yp, *mx_w, a_conv_w[i], b_ln_g[i], b_ln_b[i], b_ws[i], b_bias[i].T, *conv_ln,
            tile=PROMPT_TILE)
        yp = _ffn(yp.reshape(nb * seq, d), *f2, tile=PROMPT_TILE)
        yp = _embed(yp, p_prompt[i].reshape(nb * seq, -1), *em, tile=PROMPT_TILE)
        yp = yp.reshape(nb, seq, d)

        w00 = jnp.repeat(b_ws[i, :, 0, 0], hd).reshape(1, bw)
        b0 = jnp.repeat(b_bias[i, :, 0], hd).reshape(1, bw)
        ys = _ffn(ys, *f1, tile=ns)
        ys, na_s, nc_s, vr_s = _mixer_sample(
            ys, state_conv_a[i].reshape(ns, -1), state_conv_c[i].reshape(ns, -1), *mx_w,
            a_conv_w[i], b_ln_g[i], b_ln_b[i], w00, b0, *conv_ln)
        ys = _ffn(ys, *f2, tile=ns)
        ys = _embed(ys, p_sample[i].reshape(ns, -1), *em, tile=ns)

        a_p.append(na_p)
        c_p.append(nc_p)
        a_s.append(na_s.reshape(ns, A_KERNEL - 1, aw))
        c_s.append(nc_s.reshape(ns, C_KERNEL - 1, cw))
        v_s.append(vr_s.reshape(ns, 1, bw))
    return (yp, ys.reshape(ns, 1, d), jnp.stack(a_p), jnp.stack(c_p), jnp.stack(a_s),
            jnp.stack(c_s), jnp.stack(v_s))
```

```python
import functools

import jax
import jax.numpy as jnp
from jax import lax
from jax.experimental import pallas as pl
from jax.experimental.pallas import tpu as pltpu

EPS = 1e-6
CHUNK = 128
A_KERNEL = 3
C_KERNEL = 31
B_HEADS = 4

V7X_SUBLANES = 8
V7X_LANES = 128
V7X_MXU_COLS = 256
V7X_VMEM_BYTES = 64 * 1024 * 1024

PROMPT_TILE = 512
FF_COLS = V7X_MXU_COLS
CONV_ROWS = 32
A_HIST_PAD = V7X_SUBLANES
C_HIST_PAD = 4 * V7X_SUBLANES

F32 = jnp.float32
BF16 = jnp.bfloat16


def _vmem_limit(block_bytes):
    need = int(block_bytes * 1.25) + (8 << 20)
    return min(need, V7X_VMEM_BYTES - (4 << 20))


def _rms(x, g):
    return x * lax.rsqrt(jnp.mean(x * x, axis=-1, keepdims=True) + EPS) * g


def _ln(x, g, b):
    mu = jnp.mean(x, axis=-1, keepdims=True)
    xc = x - mu
    var = jnp.mean(xc * xc, axis=-1, keepdims=True)
    return xc * lax.rsqrt(var + EPS) * g + b


def _dot(a, b):
    return jnp.dot(a, b, preferred_element_type=F32)


def _resident(shape):
    zeros = (0,) * len(shape)
    return pl.BlockSpec(shape, lambda *_: zeros, pipeline_mode=pl.Buffered(1))


def _nbytes(shape, dtype):
    n = 1
    for s in shape:
        n *= s
    return n * jnp.dtype(dtype).itemsize


def _slab_store(ref, row0, val):
    for j in range(ref.shape[0]):
        ref[j, row0:row0 + val.shape[0], :] = val[:, j * V7X_LANES:(j + 1) * V7X_LANES]


def _slab_load(ref, row0, rows):
    return jnp.concatenate([ref[j, row0:row0 + rows, :] for j in range(ref.shape[0])], axis=1)


def _causal_conv(ref, w_ref, row0, rows):
    out = []
    for j in range(ref.shape[0]):
        lanes = slice(j * V7X_LANES, (j + 1) * V7X_LANES)
        acc = w_ref[0:1, lanes] * ref[j, row0:row0 + rows, :]
        for k in range(1, w_ref.shape[0]):
            acc = acc + w_ref[k:k + 1, lanes] * ref[j, row0 + k:row0 + k + rows, :]
        out.append(acc)
    return jnp.concatenate(out, axis=1)


def _ffn_body(*refs, with_embed):
    if with_embed:
        (x_ref, p_ref, pre_ref, post_ref, wg_ref, wu_ref, wd_ref,
         epre_ref, epost_ref, ewg_ref, ewp_ref, o_ref, act_ref) = refs
    else:
        x_ref, pre_ref, post_ref, wg_ref, wu_ref, wd_ref, o_ref, act_ref = refs
    x = x_ref[...]
    h = _rms(x, pre_ref[...]).astype(BF16)
    d_ff = wg_ref.shape[1]
    for c0 in range(0, d_ff, FF_COLS):
        g = _dot(h, wg_ref[:, c0:c0 + FF_COLS])
        u = _dot(h, wu_ref[:, c0:c0 + FF_COLS])
        act_ref[:, c0:c0 + FF_COLS] = (jax.nn.silu(g) * u).astype(BF16)
    y = _dot(act_ref[...], wd_ref[...])
    x = x + 0.5 * _rms(y, post_ref[...])
    if with_embed:
        h = _rms(x, epre_ref[...]).astype(BF16)
        gate = jax.nn.sigmoid(_dot(h, ewg_ref[...]))
        pe = _dot(p_ref[...].astype(BF16), ewp_ref[...])
        x = x + _rms(gate * pe, epost_ref[...])
    o_ref[...] = x


def _ffn(x, ffn_w, tile, p=None, embed_w=None):
    pre, post, wg, wu, wd = ffn_w
    n, d = x.shape
    d_ff = wg.shape[1]
    assert n % tile == 0 and d_ff % FF_COLS == 0
    row = pl.BlockSpec((tile, d), lambda i: (i, 0))
    args = [x, pre, post, wg, wu, wd]
    in_specs = [row] + [_resident(a.shape) for a in args[1:]]
    blocks = (4 * _nbytes((tile, d), F32) + 3 * _nbytes((d, d_ff), BF16)
              + _nbytes((tile, d_ff), BF16))
    if embed_w is not None:
        dp = p.shape[1]
        args = [x, p] + args[1:] + list(embed_w)
        in_specs = ([row, pl.BlockSpec((tile, dp), lambda i: (i, 0))] + in_specs[1:]
                    + [_resident(a.shape) for a in embed_w])
        blocks += (2 * _nbytes((tile, dp), F32) + _nbytes((d, d), BF16) + _nbytes((dp, d), BF16))
    return pl.pallas_call(
        functools.partial(_ffn_body, with_embed=embed_w is not None),
        grid=(n // tile,),
        in_specs=in_specs,
        out_specs=row,
        out_shape=jax.ShapeDtypeStruct((n, d), F32),
        scratch_shapes=[pltpu.VMEM((tile, d_ff), BF16)],
        compiler_params=pltpu.CompilerParams(
            dimension_semantics=("arbitrary",), vmem_limit_bytes=_vmem_limit(blocks)),
        name="ffn_embed" if embed_w is not None else "ffn",
    )(*args)


def _mixer_prompt_body(x_ref, pre_ref, post_ref, win_ref, wout_ref, acw_ref, lng_ref, lnb_ref,
                       ws_ref, bias_ref, ccw_ref, ccb_ref, clg_ref, clb_ref,
                       o_ref, na_ref, nc_ref, fa_ref, fc_ref, mix_ref):
    tt = x_ref.shape[1]
    aw = acw_ref.shape[1]
    bw = ws_ref.shape[0] * ws_ref.shape[1]
    hd = ws_ref.shape[1]
    cw = ccw_ref.shape[1]
    t = pl.program_id(1)

    @pl.when(t == 0)
    def _():
        fa_ref[:, 0:A_HIST_PAD, :] = jnp.zeros((fa_ref.shape[0], A_HIST_PAD, V7X_LANES), F32)
        fc_ref[:, 0:C_HIST_PAD, :] = jnp.zeros((fc_ref.shape[0], C_HIST_PAD, V7X_LANES), F32)

    x = x_ref[0]
    h = _rms(x, pre_ref[...]).astype(BF16)


    o_b = 3 * aw
    o_c = o_b + 2 * bw
    zc = _dot(h, win_ref[:, o_c:o_c + 2 * cw])
    _slab_store(fc_ref, C_HIST_PAD, zc[:, 0:cw] * jax.nn.sigmoid(zc[:, cw:2 * cw]))
    base = C_HIST_PAD - (C_KERNEL - 1)
    for r0 in range(0, tt, CONV_ROWS):
        acc = _causal_conv(fc_ref, ccw_ref, base + r0, CONV_ROWS) + ccb_ref[...]
        y_c = jax.nn.silu(_ln(acc, clg_ref[...], clb_ref[...]))
        mix_ref[r0:r0 + CONV_ROWS, aw + bw:aw + bw + cw] = y_c.astype(BF16)
    last_c = _slab_load(fc_ref, base + tt, C_KERNEL - 1)
    nc_ref[0] = last_c
    _slab_store(fc_ref, base, last_c)

    za = _dot(h, win_ref[:, 0:3 * aw])
    _slab_store(fa_ref, A_HIST_PAD, za[:, aw:2 * aw] * za[:, 0:aw])
    base = A_HIST_PAD - (A_KERNEL - 1)
    for r0 in range(0, tt, CONV_ROWS):
        y_a = za[r0:r0 + CONV_ROWS, 2 * aw:3 * aw] * _causal_conv(fa_ref, acw_ref, base + r0, CONV_ROWS)
        mix_ref[r0:r0 + CONV_ROWS, 0:aw] = y_a.astype(BF16)
    last_a = _slab_load(fa_ref, base + tt, A_KERNEL - 1)
    na_ref[0] = last_a
    _slab_store(fa_ref, base, last_a)

    u =jax.nn.gelu(_dot(h, win_ref[:, o_b:o_b + bw]))
    gv = jax.nn.gelu(_dot(h, win_ref[:, o_b + bw:o_b + 2 * bw]))
    row_i = lax.broadcasted_iota(jnp.int32, (CHUNK, CHUNK), 0)
    col_j = lax.broadcasted_iota(jnp.int32, (CHUNK, CHUNK), 1)
    n_chunks = tt // CHUNK
    for hh in range(B_HEADS):
        cols = slice(hh * hd, (hh + 1) * hd)
        v = _ln(gv[:, cols], lng_ref[hh:hh + 1, :], lnb_ref[hh:hh + 1, :]).astype(BF16)
        w_tril = jnp.where(col_j <= row_i, ws_ref[hh], 0.0).astype(BF16)
        bias = bias_ref[:, hh:hh + 1]
        v_cat = jnp.concatenate([v[c * CHUNK:(c + 1) * CHUNK, :] for c in range(n_chunks)], axis=1)
        sp = _dot(w_tril, v_cat)
        for c in range(n_chunks):
            rows = slice(c * CHUNK, (c + 1) * CHUNK)
            y_b = u[rows, cols] * (sp[:, c * hd:(c + 1) * hd] + bias)
            mix_ref[rows, aw + hh * hd:aw + (hh + 1) * hd] = y_b.astype(BF16)

    mix = _dot(mix_ref[...], wout_ref[...])
    o_ref[0] = x + _rms(mix, post_ref[...])


def _mixer_prompt(x, pre, post, w_in, w_out, a_cw, lng, lnb, ws, bias_t, c_cw, c_cb, clg, clb, tile):
    b, s, d = x.shape
    d_in = w_in.shape[1]
    d_mix = w_out.shape[0]
    aw, cw = a_cw.shape[1], c_cw.shape[1]
    assert s % tile == 0 and tile % CHUNK == 0 and tile % CONV_ROWS == 0
    assert ws.shape[1] == CHUNK and ws.shape[2] == CHUNK
    assert aw % V7X_LANES == 0 and cw % V7X_LANES == 0
    xblk = pl.BlockSpec((1, tile, d), lambda i, t: (i, t, 0))
    small = [pre, post]
    params = [a_cw, lng, lnb, ws, bias_t, c_cw, c_cb, clg, clb]
    blocks = (4 * _nbytes((tile, d), F32) + _nbytes((d, d_in), BF16) + _nbytes((d_mix, d), BF16)
              + _nbytes((tile, d_mix), BF16) + 2 * _nbytes((tile, d_in), F32))
    return pl.pallas_call(
        _mixer_prompt_body,
        grid=(b, s // tile),
        in_specs=[xblk] + [_resident(a.shape) for a in small]
                 + [_resident(w_in.shape), _resident(w_out.shape)]
                 + [_resident(a.shape) for a in params],
        out_specs=[xblk,
                   pl.BlockSpec((1, A_KERNEL - 1, aw), lambda i, t: (i, 0, 0)),
                   pl.BlockSpec((1, C_KERNEL - 1, cw), lambda i, t: (i, 0, 0))],
        out_shape=[jax.ShapeDtypeStruct((b, s, d), F32),
                   jax.ShapeDtypeStruct((b, A_KERNEL - 1, aw), F32),
                   jax.ShapeDtypeStruct((b, C_KERNEL - 1, cw), F32)],
        scratch_shapes=[pltpu.VMEM((aw // V7X_LANES, A_HIST_PAD + tile, V7X_LANES), F32),
                        pltpu.VMEM((cw // V7X_LANES, C_HIST_PAD + tile, V7X_LANES), F32),
                        pltpu.VMEM((tile, d_mix), BF16)],
        compiler_params=pltpu.CompilerParams(
            dimension_semantics=("arbitrary", "arbitrary"),
            vmem_limit_bytes=_vmem_limit(blocks)),
        name="mixer_prompt",
    )(x, *small, w_in, w_out, *params)


def _mixer_sample_body(x_ref, sa_ref, sc_ref, pre_ref, post_ref, win_ref, wout_ref, acw_ref,
                       lng_ref, lnb_ref, w00_ref, b0_ref, ccw_ref, ccb_ref, clg_ref, clb_ref,
                       o_ref, na_ref, nc_ref, v_ref, mix_ref):
    aw = acw_ref.shape[1]
    bw = w00_ref.shape[1]
    hd = bw // B_HEADS
    cw = ccw_ref.shape[1]
    x = x_ref[...]
    h = _rms(x, pre_ref[...]).astype(BF16)

    za = _dot(h, win_ref[:, 0:3 * aw])
    fa_new = za[:, aw:2 * aw] * za[:, 0:aw]
    conv_a = acw_ref[A_KERNEL - 1:A_KERNEL, :] * fa_new
    for k in range(A_KERNEL - 1):
        conv_a = conv_a + acw_ref[k:k + 1, :] * sa_ref[:, k * aw:(k + 1) * aw]
    mix_ref[:, 0:aw] = (za[:, 2 * aw:3 * aw] * conv_a).astype(BF16)
    na_ref[:, 0:(A_KERNEL - 2) * aw] = sa_ref[:, aw:(A_KERNEL - 1) * aw]
    na_ref[:, (A_KERNEL - 2) * aw:(A_KERNEL - 1) * aw] = fa_new

    o_b = 3 * aw
    u = jax.nn.gelu(_dot(h, win_ref[:, o_b:o_b + bw]))
    gv = jax.nn.gelu(_dot(h, win_ref[:, o_b + bw:o_b + 2 * bw]))
    for hh in range(B_HEADS):
        cols = slice(hh * hd, (hh + 1) * hd)
        v = _ln(gv[:, cols], lng_ref[hh:hh + 1, :], lnb_ref[hh:hh + 1, :])
        v_ref[:, cols] = v
        sp = w00_ref[:, cols] * v + b0_ref[:, cols]
        mix_ref[:, aw + hh * hd:aw + (hh + 1) * hd] = (u[:, cols] * sp).astype(BF16)

    o_c = o_b + 2 * bw
    zc = _dot(h, win_ref[:, o_c:o_c + 2 * cw])
    fc_new = zc[:, 0:cw] * jax.nn.sigmoid(zc[:, cw:2 * cw])
    acc = ccw_ref[C_KERNEL - 1:C_KERNEL, :] * fc_new + ccb_ref[...]
    for k in range(C_KERNEL - 1):
        acc = acc + ccw_ref[k:k + 1, :] * sc_ref[:, k * cw:(k + 1) * cw]
    y_c = jax.nn.silu(_ln(acc, clg_ref[...], clb_ref[...]))
    mix_ref[:, aw + bw:aw + bw + cw] = y_c.astype(BF16)
    nc_ref[:, 0:(C_KERNEL - 2) * cw] = sc_ref[:, cw:(C_KERNEL - 1) * cw]
    nc_ref[:, (C_KERNEL - 2) * cw:(C_KERNEL - 1) * cw] = fc_new

    mix = _dot(mix_ref[...], wout_ref[...])
    o_ref[...] = x + _rms(mix, post_ref[...])


def _mixer_sample(x, sa, sc, pre, post, w_in, w_out, a_cw, lng, lnb, w00, b0, c_cw, c_cb, clg, clb):
    n, d = x.shape
    d_mix = w_out.shape[0]
    bw = w00.shape[1]
    args = [x, sa, sc, pre, post, w_in, w_out, a_cw, lng, lnb, w00, b0, c_cw, c_cb, clg, clb]
    blocks = 2 * sum(_nbytes(a.shape, a.dtype) for a in args) + 2 * _nbytes(sc.shape, F32)
    return pl.pallas_call(
        _mixer_sample_body,
        out_shape=[jax.ShapeDtypeStruct(x.shape, F32), jax.ShapeDtypeStruct(sa.shape, F32),
                   jax.ShapeDtypeStruct(sc.shape, F32), jax.ShapeDtypeStruct((n, bw), F32)],
        scratch_shapes=[pltpu.VMEM((n, d_mix), BF16)],
        compiler_params=pltpu.CompilerParams(vmem_limit_bytes=_vmem_limit(blocks)),
        name="mixer_sample",
    )(*args)


def kernel(x_prompt, x_sample, state_conv_a, state_conv_c, p_prompt, p_sample, f1_pre, f1_post, f1_wg, f1_wu, f1_wd, m_pre, m_post, w_in, w_out, a_conv_w, b_ln_g, b_ln_b, b_ws, b_bias, c_conv_w, c_conv_b, c_ln_g, c_ln_b, f2_pre, f2_post, f2_wg, f2_wu, f2_wd, e_pre, e_post, e_wg, e_wp):
    depth = f1_wg.shape[0]
    nb, seq, d = x_prompt.shape
    ns = x_sample.shape[0]
    assert x_sample.shape[1] == 1
    aw, cw = a_conv_w.shape[2], c_conv_w.shape[2]
    hd = b_ws.shape[2]
    bw = B_HEADS * hd

    yp = x_prompt
    ys = x_sample.reshape(ns, d)
    a_p, c_p, a_s, c_s, v_s = [], [], [], [], []
    for i in range(depth):
        row = lambda g: g[i].reshape(1, -1)
        f1 = (row(f1_pre), row(f1_post), f1_wg[i].astype(BF16), f1_wu[i].astype(BF16),
              f1_wd[i].astype(BF16))
        f2 = (row(f2_pre), row(f2_post), f2_wg[i].astype(BF16), f2_wu[i].astype(BF16),
              f2_wd[i].astype(BF16))
        em = (row(e_pre), row(e_post), e_wg[i].astype(BF16), e_wp[i].astype(BF16))
        mx_w = (row(m_pre), row(m_post), w_in[i].astype(BF16), w_out[i].astype(BF16))
        conv_ln = (c_conv_w[i], row(c_conv_b), row(c_ln_g), row(c_ln_b))

        yp = _ffn(yp.reshape(nb * seq, d), f1, tile=PROMPT_TILE).reshape(nb, seq, d)
        yp, na_p, nc_p = _mixer_prompt(
            yp, *mx_w, a_conv_w[i], b_ln_g[i], b_ln_b[i], b_ws[i], b_bias[i].T, *conv_ln,
            tile=PROMPT_TILE)
        yp = _ffn(yp.reshape(nb * seq, d), f2, tile=PROMPT_TILE,
                  p=p_prompt[i].reshape(nb * seq, -1), embed_w=em).reshape(nb, seq, d)

        w00 = jnp.repeat(b_ws[i, :, 0, 0], hd).reshape(1, bw)
        b0 = jnp.repeat(b_bias[i, :, 0], hd).reshape(1, bw)
        ys = _ffn(ys, f1, tile=ns)
        ys, na_s, nc_s, vr_s = _mixer_sample(
            ys, state_conv_a[i].reshape(ns, -1), state_conv_c[i].reshape(ns, -1), *mx_w,
            a_conv_w[i], b_ln_g[i], b_ln_b[i], w00, b0, *conv_ln)
        ys = _ffn(ys, f2, tile=ns, p=p_sample[i].reshape(ns, -1), embed_w=em)

        a_p.append(na_p)
        c_p.append(nc_p)
        a_s.append(na_s.reshape(ns, A_KERNEL - 1, aw))
        c_s.append(nc_s.reshape(ns, C_KERNEL - 1, cw))
        v_s.append(vr_s.reshape(ns, 1, bw))
    return (yp, ys.reshape(ns, 1, d), jnp.stack(a_p), jnp.stack(c_p), jnp.stack(a_s),
            jnp.stack(c_s), jnp.stack(v_s))
```

```python
import functools
from typing import NamedTuple

import jax
import jax.numpy as jnp
from jax import lax
from jax.experimental import pallas as pl
from jax.experimental.pallas import tpu as pltpu

EPS = 1e-6
CHUNK = 128
A_KERNEL = 3
C_KERNEL = 31
B_HEADS = 4

V7X_SUBLANES = 8
V7X_LANES = 128
V7X_MXU_COLS = 256
V7X_VMEM_BYTES = 64 * 1024 * 1024

PROMPT_TILE = 512
FF_COLS = V7X_MXU_COLS
CONV_ROWS = 32
A_HIST_PAD = V7X_SUBLANES
C_HIST_PAD = 4 * V7X_SUBLANES

F32 = jnp.float32
BF16 = jnp.bfloat16


def _vmem_limit(block_bytes):
    need = int(block_bytes * 1.25) + (8 << 20)
    return min(need, V7X_VMEM_BYTES - (4 << 20))


def _rms(x, g):
    return x * lax.rsqrt(jnp.mean(x * x, axis=-1, keepdims=True) + EPS) * g


def _ln(x, g, b):
    mu = jnp.mean(x, axis=-1, keepdims=True)
    xc = x - mu
    var = jnp.mean(xc * xc, axis=-1, keepdims=True)
    return xc * lax.rsqrt(var + EPS) * g + b


def _dot(a, b):
    return jnp.dot(a, b, preferred_element_type=F32)


class _Layer(NamedTuple):
    arr: jax.Array
    layer: int


def _shape(a):
    return a.arr.shape[1:] if isinstance(a, _Layer) else a.shape


def _arr(a):
    return a.arr if isinstance(a, _Layer) else a


def _resident(a):
    shape = _shape(a)
    zeros = (0,) * len(shape)
    if isinstance(a, _Layer):
        return pl.BlockSpec((None,) + shape, lambda *_: (a.layer,) + zeros,
                            pipeline_mode=pl.Buffered(1))
    return pl.BlockSpec(shape, lambda *_: zeros, pipeline_mode=pl.Buffered(1))


def _rows(a, tile):
    cols = _shape(a)[1]
    if isinstance(a, _Layer):
        return pl.BlockSpec((None, tile, cols), lambda i: (a.layer, i, 0))
    return pl.BlockSpec((tile, cols), lambda i: (i, 0))


def _nbytes(shape, dtype):
    n = 1
    for s in shape:
        n *= s
    return n * jnp.dtype(dtype).itemsize


def _slab_store(ref, row0, val):
    for j in range(ref.shape[0]):
        ref[j, row0:row0 + val.shape[0], :] = val[:, j * V7X_LANES:(j + 1) * V7X_LANES]


def _slab_load(ref, row0, rows):
    return jnp.concatenate([ref[j, row0:row0 + rows, :] for j in range(ref.shape[0])], axis=1)


def _causal_conv(ref, w_ref, row0, rows):
    out = []
    for j in range(ref.shape[0]):
        lanes = slice(j * V7X_LANES, (j + 1) * V7X_LANES)
        acc = w_ref[0:1, lanes] * ref[j, row0:row0 + rows, :]
        for k in range(1, w_ref.shape[0]):
            acc = acc + w_ref[k:k + 1, lanes] * ref[j, row0 + k:row0 + k + rows, :]
        out.append(acc)
    return jnp.concatenate(out, axis=1)


def _ffn_body(*refs, with_embed):
    if with_embed:
        (x_ref, p_ref, pre_ref, post_ref, wg_ref, wu_ref, wd_ref,
         epre_ref, epost_ref, ewg_ref, ewp_ref, o_ref, act_ref) = refs
    else:
        x_ref, pre_ref, post_ref, wg_ref, wu_ref, wd_ref, o_ref, act_ref = refs
    x = x_ref[...]
    h = _rms(x, pre_ref[...]).astype(BF16)
    d_ff = wg_ref.shape[1]
    for c0 in range(0, d_ff, FF_COLS):
        g = _dot(h, wg_ref[:, c0:c0 + FF_COLS])
        u = _dot(h, wu_ref[:, c0:c0 + FF_COLS])
        act_ref[:, c0:c0 + FF_COLS] = (jax.nn.silu(g) * u).astype(BF16)
    y = _dot(act_ref[...], wd_ref[...])
    x = x + 0.5 * _rms(y, post_ref[...])
    if with_embed:
        h = _rms(x, epre_ref[...]).astype(BF16)
        gate = jax.nn.sigmoid(_dot(h, ewg_ref[...]))
        pe = _dot(p_ref[...].astype(BF16), ewp_ref[...])
        x = x + _rms(gate * pe, epost_ref[...])
    o_ref[...] = x


def _ffn(x, ffn_w, tile, p=None, embed_w=None):
    pre, post, wg, wu, wd = ffn_w
    n, d = x.shape
    d_ff = _shape(wg)[1]
    assert n % tile == 0 and d_ff % FF_COLS == 0
    row = _rows(x, tile)
    args = [x, pre, post, wg, wu, wd]
    in_specs = [row] + [_resident(a) for a in args[1:]]
    blocks = (4 * _nbytes((tile, d), F32) + 3 * _nbytes((d, d_ff), BF16)
              + _nbytes((tile, d_ff), BF16))
    if embed_w is not None:
        dp = _shape(p)[1]
        args = [x, p] + args[1:] + list(embed_w)
        in_specs = [row, _rows(p, tile)] + in_specs[1:] + [_resident(a) for a in embed_w]
        blocks += (2 * _nbytes((tile, dp), F32) + _nbytes((d, d), BF16) + _nbytes((dp, d), BF16))
    return pl.pallas_call(
        functools.partial(_ffn_body, with_embed=embed_w is not None),
        grid=(n // tile,),
        in_specs=in_specs,
        out_specs=row,
        out_shape=jax.ShapeDtypeStruct((n, d), F32),
        scratch_shapes=[pltpu.VMEM((tile, d_ff), BF16)],
        compiler_params=pltpu.CompilerParams(
            dimension_semantics=("arbitrary",), vmem_limit_bytes=_vmem_limit(blocks)),
        name="ffn_embed" if embed_w is not None else "ffn",
    )(*[_arr(a) for a in args])


def _mixer_prompt_body(x_ref, pre_ref, post_ref, win_ref, wout_ref, acw_ref, lng_ref, lnb_ref,
                       ws_ref, bias_ref, ccw_ref, ccb_ref, clg_ref, clb_ref,
                       o_ref, na_ref, nc_ref, fa_ref, fc_ref, mix_ref):
    tt = x_ref.shape[1]
    aw = acw_ref.shape[1]
    bw = ws_ref.shape[0] * ws_ref.shape[1]
    hd = ws_ref.shape[1]
    cw = ccw_ref.shape[1]
    t = pl.program_id(1)

    @pl.when(t == 0)
    def _():
        fa_ref[:, 0:A_HIST_PAD, :] = jnp.zeros((fa_ref.shape[0], A_HIST_PAD, V7X_LANES), F32)
        fc_ref[:, 0:C_HIST_PAD, :] = jnp.zeros((fc_ref.shape[0], C_HIST_PAD, V7X_LANES), F32)

    x = x_ref[0]
    h = _rms(x, pre_ref[...]).astype(BF16)

    o_b = 3 * aw
    o_c = o_b + 2 * bw
    zc = _dot(h, win_ref[:, o_c:o_c + 2 * cw])
    _slab_store(fc_ref, C_HIST_PAD, zc[:, 0:cw] * jax.nn.sigmoid(zc[:, cw:2 * cw]))
    base = C_HIST_PAD - (C_KERNEL - 1)
    for r0 in range(0, tt, CONV_ROWS):
        acc = _causal_conv(fc_ref, ccw_ref, base + r0, CONV_ROWS) + ccb_ref[...]
        y_c = jax.nn.silu(_ln(acc, clg_ref[...], clb_ref[...]))
        mix_ref[r0:r0 + CONV_ROWS, aw + bw:aw + bw + cw] = y_c.astype(BF16)
    last_c = _slab_load(fc_ref, base + tt, C_KERNEL - 1)
    nc_ref[0] = last_c
    _slab_store(fc_ref, base, last_c)

    za = _dot(h, win_ref[:, 0:3 * aw])
    _slab_store(fa_ref, A_HIST_PAD, za[:, aw:2 * aw] * za[:, 0:aw])
    base = A_HIST_PAD - (A_KERNEL - 1)
    for r0 in range(0, tt, CONV_ROWS):
        y_a = za[r0:r0 + CONV_ROWS, 2 * aw:3 * aw] * _causal_conv(fa_ref, acw_ref, base + r0, CONV_ROWS)
        mix_ref[r0:r0 + CONV_ROWS, 0:aw] = y_a.astype(BF16)
    last_a = _slab_load(fa_ref, base + tt, A_KERNEL - 1)
    na_ref[0] = last_a
    _slab_store(fa_ref, base, last_a)

    u = jax.nn.gelu(_dot(h, win_ref[:, o_b:o_b + bw]))
    gv = jax.nn.gelu(_dot(h, win_ref[:, o_b + bw:o_b + 2 * bw]))
    row_i = lax.broadcasted_iota(jnp.int32, (CHUNK, CHUNK), 0)
    col_j = lax.broadcasted_iota(jnp.int32, (CHUNK, CHUNK), 1)
    n_chunks = tt // CHUNK
    for hh in range(B_HEADS):
        cols = slice(hh * hd, (hh + 1) * hd)
        v = _ln(gv[:, cols], lng_ref[hh:hh + 1, :], lnb_ref[hh:hh + 1, :]).astype(BF16)
        w_tril = jnp.where(col_j <= row_i, ws_ref[hh], 0.0).astype(BF16)
        bias = bias_ref[:, hh:hh + 1]
        v_cat = jnp.concatenate([v[c * CHUNK:(c + 1) * CHUNK, :] for c in range(n_chunks)], axis=1)
        sp = _dot(w_tril, v_cat)
        for c in range(n_chunks):
            rows = slice(c * CHUNK, (c + 1) * CHUNK)
            y_b = u[rows, cols] * (sp[:, c * hd:(c + 1) * hd] + bias)
            mix_ref[rows, aw + hh * hd:aw + (hh + 1) * hd] = y_b.astype(BF16)

    mix = _dot(mix_ref[...], wout_ref[...])
    o_ref[0] = x + _rms(mix, post_ref[...])


def _mixer_prompt(x, pre, post, w_in, w_out, a_cw, lng, lnb, ws, bias_t, c_cw, c_cb, clg, clb, tile):
    b, s, d = x.shape
    d_in = _shape(w_in)[1]
    d_mix = _shape(w_out)[0]
    aw, cw = a_cw.shape[1], c_cw.shape[1]
    assert s % tile == 0 and tile % CHUNK == 0 and tile % CONV_ROWS == 0
    assert ws.shape[1] == CHUNK and ws.shape[2] == CHUNK
    assert aw % V7X_LANES == 0 and cw % V7X_LANES == 0
    xblk = pl.BlockSpec((1, tile, d), lambda i, t: (i, t, 0))
    small = [pre, post]
    params = [a_cw, lng, lnb, ws, bias_t, c_cw, c_cb, clg, clb]
    blocks = (4 * _nbytes((tile, d), F32) + _nbytes((d, d_in), BF16) + _nbytes((d_mix, d), BF16)
              + _nbytes((tile, d_mix), BF16) + 2 * _nbytes((tile, d_in), F32))
    return pl.pallas_call(
        _mixer_prompt_body,
        grid=(b, s // tile),
        in_specs=[xblk] + [_resident(a) for a in small + [w_in, w_out] + params],
        out_specs=[xblk,
                   pl.BlockSpec((1, A_KERNEL - 1, aw), lambda i, t: (i, 0, 0)),
                   pl.BlockSpec((1, C_KERNEL - 1, cw), lambda i, t: (i, 0, 0))],
        out_shape=[jax.ShapeDtypeStruct((b, s, d), F32),
                   jax.ShapeDtypeStruct((b, A_KERNEL - 1, aw), F32),
                   jax.ShapeDtypeStruct((b, C_KERNEL - 1, cw), F32)],
        scratch_shapes=[pltpu.VMEM((aw // V7X_LANES, A_HIST_PAD + tile, V7X_LANES), F32),
                        pltpu.VMEM((cw // V7X_LANES, C_HIST_PAD + tile, V7X_LANES), F32),
                        pltpu.VMEM((tile, d_mix), BF16)],
        compiler_params=pltpu.CompilerParams(
            dimension_semantics=("arbitrary", "arbitrary"),
            vmem_limit_bytes=_vmem_limit(blocks)),
        name="mixer_prompt",
    )(x, *small, _arr(w_in), _arr(w_out), *params)


def _mixer_sample_body(x_ref, sa_ref, sc_ref, pre_ref, post_ref, win_ref, wout_ref, acw_ref,
                       lng_ref, lnb_ref, w00_ref, b0_ref, ccw_ref, ccb_ref, clg_ref, clb_ref,
                       o_ref, na_ref, nc_ref, v_ref, mix_ref):
    aw = acw_ref.shape[1]
    bw = w00_ref.shape[1]
    hd = bw // B_HEADS
    cw = ccw_ref.shape[1]
    x = x_ref[...]
    h = _rms(x, pre_ref[...]).astype(BF16)

    za = _dot(h, win_ref[:, 0:3 * aw])
    fa_new = za[:, aw:2 * aw] * za[:, 0:aw]
    conv_a = acw_ref[A_KERNEL - 1:A_KERNEL, :] * fa_new
    for k in range(A_KERNEL - 1):
        conv_a = conv_a + acw_ref[k:k + 1, :] * sa_ref[:, k * aw:(k + 1) * aw]
    mix_ref[:, 0:aw] = (za[:, 2 * aw:3 * aw] * conv_a).astype(BF16)
    na_ref[:, 0:(A_KERNEL - 2) * aw] = sa_ref[:, aw:(A_KERNEL - 1) * aw]
    na_ref[:, (A_KERNEL - 2) * aw:(A_KERNEL - 1) * aw] = fa_new

    o_b = 3 * aw
    u = jax.nn.gelu(_dot(h, win_ref[:, o_b:o_b + bw]))
    gv = jax.nn.gelu(_dot(h, win_ref[:, o_b + bw:o_b + 2 * bw]))
    for hh in range(B_HEADS):
        cols = slice(hh * hd, (hh + 1) * hd)
        v = _ln(gv[:, cols], lng_ref[hh:hh + 1, :], lnb_ref[hh:hh + 1, :])
        v_ref[:, cols] = v
        sp = w00_ref[:, cols] * v + b0_ref[:, cols]
        mix_ref[:, aw + hh * hd:aw + (hh + 1) * hd] = (u[:, cols] * sp).astype(BF16)

    o_c = o_b + 2 * bw
    zc = _dot(h, win_ref[:, o_c:o_c + 2 * cw])
    fc_new = zc[:, 0:cw] * jax.nn.sigmoid(zc[:, cw:2 * cw])
    acc = ccw_ref[C_KERNEL - 1:C_KERNEL, :] * fc_new + ccb_ref[...]
    for k in range(C_KERNEL - 1):
        acc = acc + ccw_ref[k:k + 1, :] * sc_ref[:, k * cw:(k + 1) * cw]
    y_c = jax.nn.silu(_ln(acc, clg_ref[...], clb_ref[...]))
    mix_ref[:, aw + bw:aw + bw + cw] = y_c.astype(BF16)
    nc_ref[:, 0:(C_KERNEL - 2) * cw] = sc_ref[:, cw:(C_KERNEL - 1) * cw]
    nc_ref[:, (C_KERNEL - 2) * cw:(C_KERNEL - 1) * cw] = fc_new

    mix = _dot(mix_ref[...], wout_ref[...])
    o_ref[...] = x + _rms(mix, post_ref[...])


def _mixer_sample(x, sa, sc, pre, post, w_in, w_out, a_cw, lng, lnb, w00, b0, c_cw, c_cb, clg, clb):
    n, d = x.shape
    d_mix = _shape(w_out)[0]
    bw = w00.shape[1]
    args = [x, sa, sc, pre, post, w_in, w_out, a_cw, lng, lnb, w00, b0, c_cw, c_cb, clg, clb]
    out_shapes = [x.shape, sa.shape, sc.shape, (n, bw)]
    blocks = (sum(_nbytes(_shape(a), _arr(a).dtype) for a in args)
              + 2 * sum(_nbytes(o, F32) for o in out_shapes))
    return pl.pallas_call(
        _mixer_sample_body,
        grid=(1,),
        in_specs=[_resident(a) for a in args],
        out_specs=[pl.BlockSpec(o, lambda i: (0, 0)) for o in out_shapes],
        out_shape=[jax.ShapeDtypeStruct(o, F32) for o in out_shapes],
        scratch_shapes=[pltpu.VMEM((n, d_mix), BF16)],
        compiler_params=pltpu.CompilerParams(
            dimension_semantics=("arbitrary",), vmem_limit_bytes=_vmem_limit(blocks)),
        name="mixer_sample",
    )(*[_arr(a) for a in args])


def kernel(x_prompt, x_sample, state_conv_a, state_conv_c, p_prompt, p_sample, f1_pre, f1_post, f1_wg, f1_wu, f1_wd, m_pre, m_post, w_in, w_out, a_conv_w, b_ln_g, b_ln_b, b_ws, b_bias, c_conv_w, c_conv_b, c_ln_g, c_ln_b, f2_pre, f2_post, f2_wg, f2_wu, f2_wd, e_pre, e_post, e_wg, e_wp):
    depth = f1_wg.shape[0]
    nb, seq, d = x_prompt.shape
    ns = x_sample.shape[0]
    assert x_sample.shape[1] == 1
    aw, cw = a_conv_w.shape[2], c_conv_w.shape[2]
    hd = b_ws.shape[2]
    bw = B_HEADS * hd

    yp = x_prompt
    ys = x_sample.reshape(ns, d)
    big = {k: v.astype(BF16) for k, v in dict(
        f1_wg=f1_wg, f1_wu=f1_wu, f1_wd=f1_wd, f2_wg=f2_wg, f2_wu=f2_wu, f2_wd=f2_wd,
        w_in=w_in, w_out=w_out, e_wg=e_wg, e_wp=e_wp).items()}
    pp = p_prompt.reshape(depth, nb * seq, -1)
    ps = p_sample.reshape(depth, ns, -1)
    a_p, c_p, a_s, c_s, v_s = [], [], [], [], []
    for i in range(depth):
        row = lambda g: g[i].reshape(1, -1)
        w = lambda k: _Layer(big[k], i)
        f1 = (row(f1_pre), row(f1_post), w("f1_wg"), w("f1_wu"), w("f1_wd"))
        f2 = (row(f2_pre), row(f2_post), w("f2_wg"), w("f2_wu"), w("f2_wd"))
        em = (row(e_pre), row(e_post), w("e_wg"), w("e_wp"))
        mx_w = (row(m_pre), row(m_post), w("w_in"), w("w_out"))
        conv_ln = (c_conv_w[i], row(c_conv_b), row(c_ln_g), row(c_ln_b))

        yp = _ffn(yp.reshape(nb * seq, d), f1, tile=PROMPT_TILE).reshape(nb, seq, d)
        yp, na_p, nc_p = _mixer_prompt(
            yp, *mx_w, a_conv_w[i], b_ln_g[i], b_ln_b[i], b_ws[i], b_bias[i].T, *conv_ln,
            tile=PROMPT_TILE)
        yp = _ffn(yp.reshape(nb * seq, d), f2, tile=PROMPT_TILE,
                  p=_Layer(pp, i), embed_w=em).reshape(nb, seq, d)

        w00 = jnp.repeat(b_ws[i, :, 0, 0], hd).reshape(1, bw)
        b0 = jnp.repeat(b_bias[i, :, 0], hd).reshape(1, bw)
        ys = _ffn(ys, f1, tile=ns)
        ys, na_s, nc_s, vr_s = _mixer_sample(
            ys, state_conv_a[i].reshape(ns, -1), state_conv_c[i].reshape(ns, -1), *mx_w,
            a_conv_w[i], b_ln_g[i], b_ln_b[i], w00, b0, *conv_ln)
        ys = _ffn(ys, f2, tile=ns, p=_Layer(ps, i), embed_w=em)

        a_p.append(na_p)
        c_p.append(nc_p)
        a_s.append(na_s.reshape(ns, A_KERNEL - 1, aw))
        c_s.append(nc_s.reshape(ns, C_KERNEL - 1, cw))
        v_s.append(vr_s.reshape(ns, 1, bw))
    return (yp, ys.reshape(ns, 1, d), jnp.stack(a_p), jnp.stack(c_p), jnp.stack(a_s),
            jnp.stack(c_s), jnp.stack(v_s))
```

```python
import functools
from typing import NamedTuple

import jax
import jax.numpy as jnp
from jax import lax
from jax.experimental import pallas as pl
from jax.experimental.pallas import tpu as pltpu

EPS = 1e-6
CHUNK = 128
A_KERNEL = 3
C_KERNEL = 31
B_HEADS = 4

V7X_SUBLANES = 8
V7X_LANES = 128
V7X_MXU_COLS = 256
V7X_VMEM_BYTES = 64 * 1024 * 1024

PROMPT_TILE = 512
FF_COLS = V7X_MXU_COLS
CONV_ROWS = 32
A_HIST_PAD = V7X_SUBLANES
C_HIST_PAD = 4 * V7X_SUBLANES

F32 = jnp.float32
BF16 = jnp.bfloat16


def _vmem_limit(block_bytes):
    need = block_bytes + (12 << 20)
    return min(need, V7X_VMEM_BYTES - (2 << 20))


def _rms(x, g):
    return x * lax.rsqrt(jnp.mean(x * x, axis=-1, keepdims=True) + EPS) * g


def _ln(x, g, b):
    mu = jnp.mean(x, axis=-1, keepdims=True)
    xc = x - mu
    var = jnp.mean(xc * xc, axis=-1, keepdims=True)
    return xc * lax.rsqrt(var + EPS) * g + b


def _dot(a, b):
    return jnp.dot(a, b, preferred_element_type=F32)


class _Layer(NamedTuple):
    arr: jax.Array
    layer: int


def _shape(a):
    return a.arr.shape[1:] if isinstance(a, _Layer) else a.shape


def _arr(a):
    return a.arr if isinstance(a, _Layer) else a


def _resident(a):
    shape = _shape(a)
    zeros = (0,) * len(shape)
    if isinstance(a, _Layer):
        return pl.BlockSpec((None,) + shape, lambda *_: (a.layer,) + zeros,
                            pipeline_mode=pl.Buffered(1))
    return pl.BlockSpec(shape, lambda *_: zeros, pipeline_mode=pl.Buffered(1))


def _rows(a, tile):
    cols = _shape(a)[1]
    if isinstance(a, _Layer):
        return pl.BlockSpec((None, tile, cols), lambda i: (a.layer, i, 0))
    return pl.BlockSpec((tile, cols), lambda i: (i, 0))


def _nbytes(shape, dtype):
    n = 1
    for s in shape:
        n *= s
    return n * jnp.dtype(dtype).itemsize


def _slab_store(ref, row0, val):
    for j in range(ref.shape[0]):
        ref[j, row0:row0 + val.shape[0], :] = val[:, j * V7X_LANES:(j + 1) * V7X_LANES]


def _slab_load(ref, row0, rows):
    return jnp.concatenate([ref[j, row0:row0 + rows, :] for j in range(ref.shape[0])], axis=1)


def _causal_conv(ref, w_ref, row0, rows):
    out = []
    for j in range(ref.shape[0]):
        lanes = slice(j * V7X_LANES, (j + 1) * V7X_LANES)
        acc = w_ref[0:1, lanes] * ref[j, row0:row0 + rows, :]
        for k in range(1, w_ref.shape[0]):
            acc = acc + w_ref[k:k + 1, lanes] * ref[j, row0 + k:row0 + k + rows, :]
        out.append(acc)
    return jnp.concatenate(out, axis=1)


def _ffn_body(*refs, with_embed):
    if with_embed:
        (x_ref, p_ref, pre_ref, post_ref, wg_ref, wu_ref, wd_ref,
         epre_ref, epost_ref, ewg_ref, ewp_ref, o_ref, act_ref) = refs
    else:
        x_ref, pre_ref, post_ref, wg_ref, wu_ref, wd_ref, o_ref, act_ref = refs
    x = x_ref[...]
    h = _rms(x, pre_ref[...]).astype(BF16)
    d_ff = wg_ref.shape[1]
    for c0 in range(0, d_ff, FF_COLS):
        g = _dot(h, wg_ref[:, c0:c0 + FF_COLS])
        u = _dot(h, wu_ref[:, c0:c0 + FF_COLS])
        act_ref[:, c0:c0 + FF_COLS] = (jax.nn.silu(g) * u).astype(BF16)
    y = _dot(act_ref[...], wd_ref[...])
    x = x + 0.5 * _rms(y, post_ref[...])
    if with_embed:
        h = _rms(x, epre_ref[...]).astype(BF16)
        gate = jax.nn.sigmoid(_dot(h, ewg_ref[...]))
        pe = _dot(p_ref[...].astype(BF16), ewp_ref[...])
        x = x + _rms(gate * pe, epost_ref[...])
    o_ref[...] = x


def _ffn(x, ffn_w, tile, p=None, embed_w=None):
    pre, post, wg, wu, wd = ffn_w
    n, d = x.shape
    d_ff = _shape(wg)[1]
    assert n % tile == 0 and d_ff % FF_COLS == 0
    row = _rows(x, tile)
    args = [x, pre, post, wg, wu, wd]
    in_specs = [row] + [_resident(a) for a in args[1:]]
    blocks = (4 * _nbytes((tile, d), F32) + 3 * _nbytes((d, d_ff), _arr(wg).dtype)
              + _nbytes((tile, d_ff), BF16))
    if embed_w is not None:
        dp = _shape(p)[1]
        args = [x, p] + args[1:] + list(embed_w)
        in_specs = [row, _rows(p, tile)] + in_specs[1:] + [_resident(a) for a in embed_w]
        blocks += 2 * _nbytes((tile, dp), F32) + sum(
            _nbytes(_shape(a), _arr(a).dtype) for a in embed_w)
    return pl.pallas_call(
        functools.partial(_ffn_body, with_embed=embed_w is not None),
        grid=(n // tile,),
        in_specs=in_specs,
        out_specs=row,
        out_shape=jax.ShapeDtypeStruct((n, d), F32),
        scratch_shapes=[pltpu.VMEM((tile, d_ff), BF16)],
        compiler_params=pltpu.CompilerParams(
            dimension_semantics=("arbitrary",), vmem_limit_bytes=_vmem_limit(blocks)),
        name="ffn_embed" if embed_w is not None else "ffn",
    )(*[_arr(a) for a in args])


def _mixer_prompt_body(x_ref, pre_ref, post_ref, win_ref, wout_ref, acw_ref, lng_ref, lnb_ref,
                       ws_ref, bias_ref, ccw_ref, ccb_ref, clg_ref, clb_ref,
                       o_ref, na_ref, nc_ref, fa_ref, fc_ref, mix_ref):
    tt = x_ref.shape[1]
    aw = acw_ref.shape[1]
    bw = ws_ref.shape[0] * ws_ref.shape[1]
    hd = ws_ref.shape[1]
    cw = ccw_ref.shape[1]
    t = pl.program_id(1)

    @pl.when(t == 0)
    def _():
        fa_ref[:, 0:A_HIST_PAD, :] = jnp.zeros((fa_ref.shape[0], A_HIST_PAD, V7X_LANES), F32)
        fc_ref[:, 0:C_HIST_PAD, :] = jnp.zeros((fc_ref.shape[0], C_HIST_PAD, V7X_LANES), F32)

    x = x_ref[0]
    h = _rms(x, pre_ref[...]).astype(BF16)

    o_b = 3 * aw
    o_c = o_b + 2 * bw
    zc = _dot(h, win_ref[:, o_c:o_c + 2 * cw])
    _slab_store(fc_ref, C_HIST_PAD, zc[:, 0:cw] * jax.nn.sigmoid(zc[:, cw:2 * cw]))
    base = C_HIST_PAD - (C_KERNEL - 1)
    for r0 in range(0, tt, CONV_ROWS):
        acc = _causal_conv(fc_ref, ccw_ref, base + r0, CONV_ROWS) + ccb_ref[...]
        y_c = jax.nn.silu(_ln(acc, clg_ref[...], clb_ref[...]))
        mix_ref[r0:r0 + CONV_ROWS, aw + bw:aw + bw + cw] = y_c.astype(BF16)
    last_c = _slab_load(fc_ref, base + tt, C_KERNEL - 1)
    nc_ref[0] = last_c
    _slab_store(fc_ref, base, last_c)

    za = _dot(h, win_ref[:, 0:3 * aw])
    _slab_store(fa_ref, A_HIST_PAD, za[:, aw:2 * aw] * za[:, 0:aw])
    base = A_HIST_PAD - (A_KERNEL - 1)
    for r0 in range(0, tt, CONV_ROWS):
        y_a = za[r0:r0 + CONV_ROWS, 2 * aw:3 * aw] * _causal_conv(fa_ref, acw_ref, base + r0, CONV_ROWS)
        mix_ref[r0:r0 + CONV_ROWS, 0:aw] = y_a.astype(BF16)
    last_a = _slab_load(fa_ref, base + tt, A_KERNEL - 1)
    na_ref[0] = last_a
    _slab_store(fa_ref, base, last_a)

    u = jax.nn.gelu(_dot(h, win_ref[:, o_b:o_b + bw]))
    gv = jax.nn.gelu(_dot(h, win_ref[:, o_b + bw:o_b + 2 * bw]))
    row_i = lax.broadcasted_iota(jnp.int32, (CHUNK, CHUNK), 0)
    col_j = lax.broadcasted_iota(jnp.int32, (CHUNK, CHUNK), 1)
    n_chunks = tt // CHUNK
    for hh in range(B_HEADS):
        cols = slice(hh * hd, (hh + 1) * hd)
        v = _ln(gv[:, cols], lng_ref[hh:hh + 1, :], lnb_ref[hh:hh + 1, :]).astype(BF16)
        w_tril = jnp.where(col_j <= row_i, ws_ref[hh], 0.0).astype(BF16)
        bias = bias_ref[:, hh:hh + 1]
        v_cat = jnp.concatenate([v[c * CHUNK:(c + 1) * CHUNK, :] for c in range(n_chunks)], axis=1)
        sp = _dot(w_tril, v_cat)
        for c in range(n_chunks):
            rows = slice(c * CHUNK, (c + 1) * CHUNK)
            y_b = u[rows, cols] * (sp[:, c * hd:(c + 1) * hd] + bias)
            mix_ref[rows, aw + hh * hd:aw + (hh + 1) * hd] = y_b.astype(BF16)

    mix = _dot(mix_ref[...], wout_ref[...])
    o_ref[0] = x + _rms(mix, post_ref[...])


def _mixer_prompt(x, pre, post, w_in, w_out, a_cw, lng, lnb, ws, bias_t, c_cw, c_cb, clg, clb, tile):
    b, s, d = x.shape
    d_in = _shape(w_in)[1]
    d_mix = _shape(w_out)[0]
    aw, cw = a_cw.shape[1], c_cw.shape[1]
    assert s % tile == 0 and tile % CHUNK == 0 and tile % CONV_ROWS == 0
    assert ws.shape[1] == CHUNK and ws.shape[2] == CHUNK
    assert aw % V7X_LANES == 0 and cw % V7X_LANES == 0
    xblk = pl.BlockSpec((1, tile, d), lambda i, t: (i, t, 0))
    small = [pre, post]
    params = [a_cw, lng, lnb, ws, bias_t, c_cw, c_cb, clg, clb]
    blocks = (4 * _nbytes((tile, d), F32) + _nbytes((d, d_in), _arr(w_in).dtype)
              + _nbytes((d_mix, d), _arr(w_out).dtype)
              + _nbytes((tile, d_mix), BF16) + 2 * _nbytes((tile, d_in), F32))
    return pl.pallas_call(
        _mixer_prompt_body,
        grid=(b, s // tile),
        in_specs=[xblk] + [_resident(a) for a in small + [w_in, w_out] + params],
        out_specs=[xblk,
                   pl.BlockSpec((1, A_KERNEL - 1, aw), lambda i, t: (i, 0, 0)),
                   pl.BlockSpec((1, C_KERNEL - 1, cw), lambda i, t: (i, 0, 0))],
        out_shape=[jax.ShapeDtypeStruct((b, s, d), F32),
                   jax.ShapeDtypeStruct((b, A_KERNEL - 1, aw), F32),
                   jax.ShapeDtypeStruct((b, C_KERNEL - 1, cw), F32)],
        scratch_shapes=[pltpu.VMEM((aw // V7X_LANES, A_HIST_PAD + tile, V7X_LANES), F32),
                        pltpu.VMEM((cw // V7X_LANES, C_HIST_PAD + tile, V7X_LANES), F32),
                        pltpu.VMEM((tile, d_mix), BF16)],
        compiler_params=pltpu.CompilerParams(
            dimension_semantics=("arbitrary", "arbitrary"),
            vmem_limit_bytes=_vmem_limit(blocks)),
        name="mixer_prompt",
    )(x, *small, _arr(w_in), _arr(w_out), *params)


def _state_conv(state_ref, w_ref, new):
    k1 = state_ref.shape[1]
    return jnp.sum(state_ref[...] * w_ref[0:k1, :][None], axis=1) + w_ref[k1:k1 + 1, :] * new


def _state_shift(out_ref, state_ref, new):
    k1 = state_ref.shape[1]
    out_ref[:, 0:k1 - 1, :] = state_ref[:, 1:k1, :]
    out_ref[:, k1 - 1:k1, :] = new[:, None, :]


def _mixer_sample_body(x_ref, sa_ref, sc_ref, pre_ref, post_ref, win_ref, wout_ref, acw_ref,
                       lng_ref, lnb_ref, w00_ref, b0_ref, ccw_ref, ccb_ref, clg_ref, clb_ref,
                       o_ref, na_ref, nc_ref, v_ref, mix_ref):
    aw = acw_ref.shape[1]
    bw = w00_ref.shape[1]
    hd = bw // B_HEADS
    cw = ccw_ref.shape[1]
    x = x_ref[...]
    h = _rms(x, pre_ref[...]).astype(BF16)

    za = _dot(h, win_ref[:, 0:3 * aw])
    fa_new = za[:, aw:2 * aw] * za[:, 0:aw]
    conv_a = _state_conv(sa_ref, acw_ref, fa_new)
    mix_ref[:, 0:aw] = (za[:, 2 * aw:3 * aw] * conv_a).astype(BF16)
    _state_shift(na_ref, sa_ref, fa_new)

    o_b = 3 * aw
    u = jax.nn.gelu(_dot(h, win_ref[:, o_b:o_b + bw]))
    gv = jax.nn.gelu(_dot(h, win_ref[:, o_b + bw:o_b + 2 * bw]))
    for hh in range(B_HEADS):
        cols = slice(hh * hd, (hh + 1) * hd)
        v = _ln(gv[:, cols], lng_ref[hh:hh + 1, :], lnb_ref[hh:hh + 1, :])
        v_ref[:, cols] = v
        sp = w00_ref[:, cols] * v + b0_ref[:, cols]
        mix_ref[:, aw + hh * hd:aw + (hh + 1) * hd] = (u[:, cols] * sp).astype(BF16)

    o_c = o_b + 2 * bw
    zc = _dot(h, win_ref[:, o_c:o_c + 2 * cw])
    fc_new = zc[:, 0:cw] * jax.nn.sigmoid(zc[:, cw:2 * cw])
    acc = _state_conv(sc_ref, ccw_ref, fc_new) + ccb_ref[...]
    y_c = jax.nn.silu(_ln(acc, clg_ref[...], clb_ref[...]))
    mix_ref[:, aw + bw:aw + bw + cw] = y_c.astype(BF16)
    _state_shift(nc_ref, sc_ref, fc_new)

    mix = _dot(mix_ref[...], wout_ref[...])
    o_ref[...] = x + _rms(mix, post_ref[...])


def _mixer_sample(x, sa, sc, pre, post, w_in, w_out, a_cw, lng, lnb, w00, b0, c_cw, c_cb, clg, clb):
    n, d = x.shape
    d_mix = _shape(w_out)[0]
    bw = w00.shape[1]
    args = [x, sa, sc, pre, post, w_in, w_out, a_cw, lng, lnb, w00, b0, c_cw, c_cb, clg, clb]
    out_shapes = [x.shape, _shape(sa), _shape(sc), (n, bw)]
    blocks = (sum(_nbytes(_shape(a), _arr(a).dtype) for a in args)
              + 2 * sum(_nbytes(o, F32) for o in out_shapes))
    return pl.pallas_call(
        _mixer_sample_body,
        grid=(1,),
        in_specs=[_resident(a) for a in args],
        out_specs=[pl.BlockSpec(o, functools.partial(lambda r, i: (0,) * r, len(o)))
                   for o in out_shapes],
        out_shape=[jax.ShapeDtypeStruct(o, F32) for o in out_shapes],
        scratch_shapes=[pltpu.VMEM((n, d_mix), BF16)],
        compiler_params=pltpu.CompilerParams(
            dimension_semantics=("arbitrary",), vmem_limit_bytes=_vmem_limit(blocks)),
        name="mixer_sample",
    )(*[_arr(a) for a in args])


def kernel(x_prompt, x_sample, state_conv_a, state_conv_c, p_prompt, p_sample, f1_pre, f1_post, f1_wg, f1_wu, f1_wd, m_pre, m_post, w_in, w_out, a_conv_w, b_ln_g, b_ln_b, b_ws, b_bias, c_conv_w, c_conv_b, c_ln_g, c_ln_b, f2_pre, f2_post, f2_wg, f2_wu, f2_wd, e_pre, e_post, e_wg, e_wp):
    depth = f1_wg.shape[0]
    nb, seq, d = x_prompt.shape
    ns = x_sample.shape[0]
    assert x_sample.shape[1] == 1
    aw, cw = a_conv_w.shape[2], c_conv_w.shape[2]
    hd = b_ws.shape[2]
    bw = B_HEADS * hd

    yp = x_prompt
    ys = x_sample.reshape(ns, d)
    pp = p_prompt.reshape(depth, nb * seq, -1)
    ps = p_sample.reshape(depth, ns, -1)
    a_p, c_p, a_s, c_s, v_s = [], [], [], [], []
    for i in range(depth):
        row = lambda g: g[i].reshape(1, -1)
        w = lambda a: _Layer(a, i)
        f1 = (row(f1_pre), row(f1_post), w(f1_wg), w(f1_wu), w(f1_wd))
        f2 = (row(f2_pre), row(f2_post), w(f2_wg), w(f2_wu), w(f2_wd))
        em = (row(e_pre), row(e_post), w(e_wg), w(e_wp))
        mx_w = (row(m_pre), row(m_post), w(w_in), w(w_out))
        conv_ln = (c_conv_w[i], row(c_conv_b), row(c_ln_g), row(c_ln_b))

        yp = _ffn(yp.reshape(nb * seq, d), f1, tile=PROMPT_TILE).reshape(nb, seq, d)
        yp, na_p, nc_p = _mixer_prompt(
            yp, *mx_w, a_conv_w[i], b_ln_g[i], b_ln_b[i], b_ws[i], b_bias[i].T, *conv_ln,
            tile=PROMPT_TILE)
        yp = _ffn(yp.reshape(nb * seq, d), f2, tile=PROMPT_TILE,
                  p=_Layer(pp, i), embed_w=em).reshape(nb, seq, d)

        w00 = jnp.repeat(b_ws[i, :, 0, 0], hd).reshape(1, bw)
        b0 = jnp.repeat(b_bias[i, :, 0], hd).reshape(1, bw)
        ys = _ffn(ys, f1, tile=ns)
        ys, na_s, nc_s, vr_s = _mixer_sample(
            ys, w(state_conv_a), w(state_conv_c), *mx_w,
            a_conv_w[i], b_ln_g[i], b_ln_b[i], w00, b0, *conv_ln)
        ys = _ffn(ys, f2, tile=ns, p=_Layer(ps, i), embed_w=em)

        a_p.append(na_p)
        c_p.append(nc_p)
        a_s.append(na_s)
        c_s.append(nc_s)
        v_s.append(vr_s.reshape(ns, 1, bw))
    return (yp, ys.reshape(ns, 1, d), jnp.stack(a_p), jnp.stack(c_p), jnp.stack(a_s),
            jnp.stack(c_s), jnp.stack(v_s))
```

```python
import functools
import math
from typing import NamedTuple

import jax
import jax.numpy as jnp
from jax import lax
from jax.experimental import pallas as pl
from jax.experimental.pallas import tpu as pltpu

EPS = 1e-6
CHUNK = 128
A_KERNEL = 3
C_KERNEL = 31
B_HEADS = 4

V7X_SUBLANES = 8
V7X_LANES = 128
V7X_MXU_COLS = 256
V7X_VMEM_BYTES = 64 * 1024 * 1024

PROMPT_TILE = 512
FFN_SUB_ROWS = 256
FF_COLS = V7X_MXU_COLS
CONV_ROWS = 32
A_HIST_PAD = V7X_SUBLANES
C_HIST_PAD = 4 * V7X_SUBLANES

GELU_C = math.sqrt(2.0 / math.pi)
GELU_CUBIC = 0.044715

F32 = jnp.float32
BF16 = jnp.bfloat16


def _vmem_limit(block_bytes):
    need = block_bytes + (12 << 20)
    return min(need, V7X_VMEM_BYTES - (2 << 20))


def _rms(x, g):
    return x * lax.rsqrt(jnp.mean(x * x, axis=-1, keepdims=True) + EPS) * g


def _ln(x, g, b):
    mu = jnp.mean(x, axis=-1, keepdims=True)
    xc = x - mu
    var = jnp.mean(xc * xc, axis=-1, keepdims=True)
    return xc * lax.rsqrt(var + EPS) * g + b


def _gelu(x):
    inner = x * (GELU_C + (GELU_C * GELU_CUBIC) * (x * x))
    hx = 0.5 * x
    return hx + hx * jnp.tanh(inner)


def _dot(a, b):
    return jnp.dot(a, b, preferred_element_type=F32)


class _Layer(NamedTuple):
    arr: jax.Array
    layer: int


def _shape(a):
    return a.arr.shape[1:] if isinstance(a, _Layer) else a.shape


def _arr(a):
    return a.arr if isinstance(a, _Layer) else a


def _resident(a):
    shape = _shape(a)
    zeros = (0,) * len(shape)
    if isinstance(a, _Layer):
        return pl.BlockSpec((None,) + shape, lambda *_: (a.layer,) + zeros,
                            pipeline_mode=pl.Buffered(1))
    return pl.BlockSpec(shape, lambda *_: zeros, pipeline_mode=pl.Buffered(1))


def _rows(a, tile):
    cols = _shape(a)[1]
    if isinstance(a, _Layer):
        return pl.BlockSpec((None, tile, cols), lambda i: (a.layer, i, 0))
    return pl.BlockSpec((tile, cols), lambda i: (i, 0))


def _nbytes(shape, dtype):
    n = 1
    for s in shape:
        n *= s
    return n * jnp.dtype(dtype).itemsize


def _slab_store(ref, row0, val):
    for j in range(ref.shape[0]):
        ref[j, row0:row0 + val.shape[0], :] = val[:, j * V7X_LANES:(j + 1) * V7X_LANES]


def _slab_load(ref, row0, rows):
    return jnp.concatenate([ref[j, row0:row0 + rows, :] for j in range(ref.shape[0])], axis=1)


def _causal_conv(ref, w_ref, row0, rows):
    out = []
    for j in range(ref.shape[0]):
        lanes = slice(j * V7X_LANES, (j + 1) * V7X_LANES)
        acc = w_ref[0:1, lanes] * ref[j, row0:row0 + rows, :]
        for k in range(1, w_ref.shape[0]):
            acc = acc + w_ref[k:k + 1, lanes] * ref[j, row0 + k:row0 + k + rows, :]
        out.append(acc)
    return jnp.concatenate(out, axis=1)


def _ffn_body(*refs, with_embed):
    if with_embed:
        (x_ref, p_ref, pre_ref, post_ref, wg_ref, wu_ref, wd_ref,
         epre_ref, epost_ref, ewg_ref, ewp_ref, o_ref, act_ref) = refs
    else:
        x_ref, pre_ref, post_ref, wg_ref, wu_ref, wd_ref, o_ref, act_ref = refs
    d_ff = wg_ref.shape[1]
    tile = x_ref.shape[0]
    sub = min(tile, FFN_SUB_ROWS)
    subs = [slice(r0, r0 + sub) for r0 in range(0, tile, sub)]
    xs = [x_ref[r, :] for r in subs]
    hs = [_rms(x, pre_ref[...]).astype(BF16) for x in xs]
    for r, h in zip(subs, hs):
        for c0 in range(0, d_ff, FF_COLS):
            g = _dot(h, wg_ref[:, c0:c0 + FF_COLS])
            u = _dot(h, wu_ref[:, c0:c0 + FF_COLS])
            act_ref[r, c0:c0 + FF_COLS] = (jax.nn.silu(g) * u).astype(BF16)
    ys = [_dot(act_ref[r, :], wd_ref[...]) for r in subs]
    xs = [x + 0.5 * _rms(y, post_ref[...]) for x, y in zip(xs, ys)]
    if with_embed:
        hs = [_rms(x, epre_ref[...]).astype(BF16) for x in xs]
        gates = [jax.nn.sigmoid(_dot(h, ewg_ref[...])) for h in hs]
        pes = [_dot(p_ref[r, :].astype(BF16), ewp_ref[...]) for r in subs]
        xs = [x + _rms(gate * pe, epost_ref[...]) for x, gate, pe in zip(xs, gates, pes)]
    for r, x in zip(subs, xs):
        o_ref[r, :] = x


def _ffn(x, ffn_w, tile, p=None, embed_w=None):
    pre, post, wg, wu, wd = ffn_w
    n, d = x.shape
    d_ff = _shape(wg)[1]
    assert n % tile == 0 and d_ff % FF_COLS == 0
    row = _rows(x, tile)
    args = [x, pre, post, wg, wu, wd]
    in_specs = [row] + [_resident(a) for a in args[1:]]
    blocks = (4 * _nbytes((tile, d), F32) + 3 * _nbytes((d, d_ff), _arr(wg).dtype)
              + _nbytes((tile, d_ff), BF16))
    if embed_w is not None:
        dp = _shape(p)[1]
        args = [x, p] + args[1:] + list(embed_w)
        in_specs = [row, _rows(p, tile)] + in_specs[1:] + [_resident(a) for a in embed_w]
        blocks += 2 * _nbytes((tile, dp), F32) + sum(
            _nbytes(_shape(a), _arr(a).dtype) for a in embed_w)
    return pl.pallas_call(
        functools.partial(_ffn_body, with_embed=embed_w is not None),
        grid=(n // tile,),
        in_specs=in_specs,
        out_specs=row,
        out_shape=jax.ShapeDtypeStruct((n, d), F32),
        scratch_shapes=[pltpu.VMEM((tile, d_ff), BF16)],
        compiler_params=pltpu.CompilerParams(
            dimension_semantics=("arbitrary",), vmem_limit_bytes=_vmem_limit(blocks)),
        name="ffn_embed" if embed_w is not None else "ffn",
    )(*[_arr(a) for a in args])


def _mixer_prompt_body(x_ref, pre_ref, post_ref, win32_ref, wout32_ref, acw_ref, lng_ref, lnb_ref,
                       ws_ref, bias_ref, ccw_ref, ccb_ref, clg_ref, clb_ref,
                       o_ref, na_ref, nc_ref, fa_ref, fc_ref, mix_ref, win_ref, wout_ref):
    tt = x_ref.shape[1]
    aw = acw_ref.shape[1]
    bw = ws_ref.shape[0] * ws_ref.shape[1]
    hd = ws_ref.shape[1]
    cw = ccw_ref.shape[1]
    t = pl.program_id(1)

    @pl.when((pl.program_id(0) == 0) & (t == 0))
    def _():
        win_ref[...] = win32_ref[...].astype(BF16)
        wout_ref[...] = wout32_ref[...].astype(BF16)

    @pl.when(t == 0)
    def _():
        fa_ref[:, 0:A_HIST_PAD, :] = jnp.zeros((fa_ref.shape[0], A_HIST_PAD, V7X_LANES), F32)
        fc_ref[:, 0:C_HIST_PAD, :] = jnp.zeros((fc_ref.shape[0], C_HIST_PAD, V7X_LANES), F32)

    x = x_ref[0]
    h = _rms(x, pre_ref[...]).astype(BF16)

    o_b = 3 * aw
    o_c = o_b + 2 * bw
    zc = _dot(h, win_ref[:, o_c:o_c + 2 * cw])
    _slab_store(fc_ref, C_HIST_PAD, zc[:, 0:cw] * jax.nn.sigmoid(zc[:, cw:2 * cw]))
    base = C_HIST_PAD - (C_KERNEL - 1)
    for r0 in range(0, tt, CONV_ROWS):
        acc = _causal_conv(fc_ref, ccw_ref, base + r0, CONV_ROWS) + ccb_ref[...]
        y_c = jax.nn.silu(_ln(acc, clg_ref[...], clb_ref[...]))
        mix_ref[r0:r0 + CONV_ROWS, aw + bw:aw + bw + cw] = y_c.astype(BF16)
    last_c = _slab_load(fc_ref, base + tt, C_KERNEL - 1)
    nc_ref[0] = last_c
    _slab_store(fc_ref, base, last_c)

    za = _dot(h, win_ref[:, 0:3 * aw])
    _slab_store(fa_ref, A_HIST_PAD, za[:, aw:2 * aw] * za[:, 0:aw])
    base = A_HIST_PAD - (A_KERNEL - 1)
    for r0 in range(0, tt, CONV_ROWS):
        y_a = za[r0:r0 + CONV_ROWS, 2 * aw:3 * aw] * _causal_conv(fa_ref, acw_ref, base + r0, CONV_ROWS)
        mix_ref[r0:r0 + CONV_ROWS, 0:aw] = y_a.astype(BF16)
    last_a = _slab_load(fa_ref, base + tt, A_KERNEL - 1)
    na_ref[0] = last_a
    _slab_store(fa_ref, base, last_a)

    u = _gelu(_dot(h, win_ref[:, o_b:o_b + bw]))
    gv = _gelu(_dot(h, win_ref[:, o_b + bw:o_b + 2 * bw]))
    row_i = lax.broadcasted_iota(jnp.int32, (CHUNK, CHUNK), 0)
    col_j = lax.broadcasted_iota(jnp.int32, (CHUNK, CHUNK), 1)
    n_chunks = tt // CHUNK
    for hh in range(B_HEADS):
        cols = slice(hh * hd, (hh + 1) * hd)
        v = _ln(gv[:, cols], lng_ref[hh:hh + 1, :], lnb_ref[hh:hh + 1, :]).astype(BF16)
        w_tril = jnp.where(col_j <= row_i, ws_ref[hh], 0.0).astype(BF16)
        bias = bias_ref[:, hh:hh + 1]
        v_cat = jnp.concatenate([v[c * CHUNK:(c + 1) * CHUNK, :] for c in range(n_chunks)], axis=1)
        sp = _dot(w_tril, v_cat)
        for c in range(n_chunks):
            rows = slice(c * CHUNK, (c + 1) * CHUNK)
            y_b = u[rows, cols] * (sp[:, c * hd:(c + 1) * hd] + bias)
            mix_ref[rows, aw + hh * hd:aw + (hh + 1) * hd] = y_b.astype(BF16)

    mix = _dot(mix_ref[...], wout_ref[...])
    o_ref[0] = x + _rms(mix, post_ref[...])


def _mixer_prompt(x, pre, post, w_in, w_out, a_cw, lng, lnb, ws, bias_t, c_cw, c_cb, clg, clb, tile):
    b, s, d = x.shape
    d_in = _shape(w_in)[1]
    d_mix = _shape(w_out)[0]
    aw, cw = a_cw.shape[1], c_cw.shape[1]
    assert s % tile == 0 and tile % CHUNK == 0 and tile % CONV_ROWS == 0
    assert ws.shape[1] == CHUNK and ws.shape[2] == CHUNK
    assert aw % V7X_LANES == 0 and cw % V7X_LANES == 0
    xblk = pl.BlockSpec((1, tile, d), lambda i, t: (i, t, 0))
    small = [pre, post]
    params = [a_cw, lng, lnb, ws, bias_t, c_cw, c_cb, clg, clb]
    blocks = (4 * _nbytes((tile, d), F32) + _nbytes((d, d_in), _arr(w_in).dtype)
              + _nbytes((d_mix, d), _arr(w_out).dtype)
              + _nbytes((d, d_in), BF16) + _nbytes((d_mix, d), BF16)
              + _nbytes((tile, d_mix), BF16) + 2 * _nbytes((tile, d_in), F32))
    return pl.pallas_call(
        _mixer_prompt_body,
        grid=(b, s // tile),
        in_specs=[xblk] + [_resident(a) for a in small + [w_in, w_out] + params],
        out_specs=[xblk,
                   pl.BlockSpec((1, A_KERNEL - 1, aw), lambda i, t: (i, 0, 0)),
                   pl.BlockSpec((1, C_KERNEL - 1, cw), lambda i, t: (i, 0, 0))],
        out_shape=[jax.ShapeDtypeStruct((b, s, d), F32),
                   jax.ShapeDtypeStruct((b, A_KERNEL - 1, aw), F32),
                   jax.ShapeDtypeStruct((b, C_KERNEL - 1, cw), F32)],
        scratch_shapes=[pltpu.VMEM((aw // V7X_LANES, A_HIST_PAD + tile, V7X_LANES), F32),
                        pltpu.VMEM((cw // V7X_LANES, C_HIST_PAD + tile, V7X_LANES), F32),
                        pltpu.VMEM((tile, d_mix), BF16),
                        pltpu.VMEM((d, d_in), BF16), pltpu.VMEM((d_mix, d), BF16)],
        compiler_params=pltpu.CompilerParams(
            dimension_semantics=("arbitrary", "arbitrary"),
            vmem_limit_bytes=_vmem_limit(blocks)),
        name="mixer_prompt",
    )(x, *small, _arr(w_in), _arr(w_out), *params)


def _state_conv(state_ref, w_ref, new):
    k1 = state_ref.shape[1]
    return jnp.sum(state_ref[...] * w_ref[0:k1, :][None], axis=1) + w_ref[k1:k1 + 1, :] * new


def _state_shift(out_ref, state_ref, new):
    k1 = state_ref.shape[1]
    out_ref[:, 0:k1 - 1, :] = state_ref[:, 1:k1, :]
    out_ref[:, k1 - 1:k1, :] = new[:, None, :]


def _mixer_sample_body(x_ref, sa_ref, sc_ref, pre_ref, post_ref, win_ref, wout_ref, acw_ref,
                       lng_ref, lnb_ref, w00_ref, b0_ref, ccw_ref, ccb_ref, clg_ref, clb_ref,
                       o_ref, na_ref, nc_ref, v_ref, mix_ref):
    aw = acw_ref.shape[1]
    bw = w00_ref.shape[1]
    hd = bw // B_HEADS
    cw = ccw_ref.shape[1]
    x = x_ref[...]
    h = _rms(x, pre_ref[...]).astype(BF16)

    za = _dot(h, win_ref[:, 0:3 * aw])
    fa_new = za[:, aw:2 * aw] * za[:, 0:aw]
    conv_a = _state_conv(sa_ref, acw_ref, fa_new)
    mix_ref[:, 0:aw] = (za[:, 2 * aw:3 * aw] * conv_a).astype(BF16)
    _state_shift(na_ref, sa_ref, fa_new)

    o_b = 3 * aw
    u = _gelu(_dot(h, win_ref[:, o_b:o_b + bw]))
    gv = _gelu(_dot(h, win_ref[:, o_b + bw:o_b + 2 * bw]))
    for hh in range(B_HEADS):
        cols = slice(hh * hd, (hh + 1) * hd)
        v = _ln(gv[:, cols], lng_ref[hh:hh + 1, :], lnb_ref[hh:hh + 1, :])
        v_ref[:, cols] = v
        sp = w00_ref[:, cols] * v + b0_ref[:, cols]
        mix_ref[:, aw + hh * hd:aw + (hh + 1) * hd] = (u[:, cols] * sp).astype(BF16)

    o_c = o_b + 2 * bw
    zc = _dot(h, win_ref[:, o_c:o_c + 2 * cw])
    fc_new = zc[:, 0:cw] * jax.nn.sigmoid(zc[:, cw:2 * cw])
    acc = _state_conv(sc_ref, ccw_ref, fc_new) + ccb_ref[...]
    y_c = jax.nn.silu(_ln(acc, clg_ref[...], clb_ref[...]))
    mix_ref[:, aw + bw:aw + bw + cw] = y_c.astype(BF16)
    _state_shift(nc_ref, sc_ref, fc_new)

    mix = _dot(mix_ref[...], wout_ref[...])
    o_ref[...] = x + _rms(mix, post_ref[...])


def _mixer_sample(x, sa, sc, pre, post, w_in, w_out, a_cw, lng, lnb, w00, b0, c_cw, c_cb, clg, clb):
    n, d = x.shape
    d_mix = _shape(w_out)[0]
    bw = w00.shape[1]
    args = [x, sa, sc, pre, post, w_in, w_out, a_cw, lng, lnb, w00, b0, c_cw, c_cb, clg, clb]
    out_shapes = [x.shape, _shape(sa), _shape(sc), (n, bw)]
    blocks = (sum(_nbytes(_shape(a), _arr(a).dtype) for a in args)
              + 2 * sum(_nbytes(o, F32) for o in out_shapes))
    return pl.pallas_call(
        _mixer_sample_body,
        grid=(1,),
        in_specs=[_resident(a) for a in args],
        out_specs=[pl.BlockSpec(o, functools.partial(lambda r, i: (0,) * r, len(o)))
                   for o in out_shapes],
        out_shape=[jax.ShapeDtypeStruct(o, F32) for o in out_shapes],
        scratch_shapes=[pltpu.VMEM((n, d_mix), BF16)],
        compiler_params=pltpu.CompilerParams(
            dimension_semantics=("arbitrary",), vmem_limit_bytes=_vmem_limit(blocks)),
        name="mixer_sample",
    )(*[_arr(a) for a in args])


def kernel(x_prompt, x_sample, state_conv_a, state_conv_c, p_prompt, p_sample, f1_pre, f1_post, f1_wg, f1_wu, f1_wd, m_pre, m_post, w_in, w_out, a_conv_w, b_ln_g, b_ln_b, b_ws, b_bias, c_conv_w, c_conv_b, c_ln_g, c_ln_b, f2_pre, f2_post, f2_wg, f2_wu, f2_wd, e_pre, e_post, e_wg, e_wp):
    depth = f1_wg.shape[0]
    nb, seq, d = x_prompt.shape
    ns = x_sample.shape[0]
    assert x_sample.shape[1] == 1
    aw, cw = a_conv_w.shape[2], c_conv_w.shape[2]
    hd = b_ws.shape[2]
    bw = B_HEADS * hd

    yp = x_prompt
    ys = x_sample.reshape(ns, d)
    pp = p_prompt.reshape(depth, nb * seq, -1)
    ps = p_sample.reshape(depth, ns, -1)
    a_p, c_p, a_s, c_s, v_s = [], [], [], [], []
    for i in range(depth):
        row = lambda g: g[i].reshape(1, -1)
        w = lambda a: _Layer(a, i)
        f1 = (row(f1_pre), row(f1_post), w(f1_wg), w(f1_wu), w(f1_wd))
        f2 = (row(f2_pre), row(f2_post), w(f2_wg), w(f2_wu), w(f2_wd))
        em = (row(e_pre), row(e_post), w(e_wg), w(e_wp))
        mx_w = (row(m_pre), row(m_post), w(w_in), w(w_out))
        conv_ln = (c_conv_w[i], row(c_conv_b), row(c_ln_g), row(c_ln_b))

        yp = _ffn(yp.reshape(nb * seq, d), f1, tile=PROMPT_TILE).reshape(nb, seq, d)
        yp, na_p, nc_p = _mixer_prompt(
            yp, *mx_w, a_conv_w[i], b_ln_g[i], b_ln_b[i], b_ws[i], b_bias[i].T, *conv_ln,
            tile=PROMPT_TILE)
        yp = _ffn(yp.reshape(nb * seq, d), f2, tile=PROMPT_TILE,
                  p=_Layer(pp, i), embed_w=em).reshape(nb, seq, d)

        w00 = jnp.repeat(b_ws[i, :, 0, 0], hd).reshape(1, bw)
        b0 = jnp.repeat(b_bias[i, :, 0], hd).reshape(1, bw)
        ys = _ffn(ys, f1, tile=ns)
        ys, na_s, nc_s, vr_s = _mixer_sample(
            ys, w(state_conv_a), w(state_conv_c), *mx_w,
            a_conv_w[i], b_ln_g[i], b_ln_b[i], w00, b0, *conv_ln)
        ys = _ffn(ys, f2, tile=ns, p=_Layer(ps, i), embed_w=em)

        a_p.append(na_p)
        c_p.append(nc_p)
        a_s.append(na_s)
        c_s.append(nc_s)
        v_s.append(vr_s.reshape(ns, 1, bw))
    return (yp, ys.reshape(ns, 1, d), jnp.stack(a_p), jnp.stack(c_p), jnp.stack(a_s),
            jnp.stack(c_s), jnp.stack(v_s))
```

```python
import functools
import math
from typing import NamedTuple

import jax
import jax.numpy as jnp
from jax import lax
from jax.experimental import pallas as pl
from jax.experimental.pallas import tpu as pltpu

EPS = 1e-6
CHUNK = 128
A_KERNEL = 3
C_KERNEL = 31
B_HEADS = 4

V7X_SUBLANES = 8
V7X_LANES = 128
V7X_MXU_COLS = 256
V7X_VMEM_BYTES = 64 * 1024 * 1024

PROMPT_TILE = 512
FFN_SUB_ROWS = 256
FF_COLS = V7X_MXU_COLS
CONV_ROWS = 32
A_HIST_PAD = V7X_SUBLANES
C_HIST_PAD = 4 * V7X_SUBLANES

GELU_C = math.sqrt(2.0 / math.pi)
GELU_CUBIC = 0.044715

F32 = jnp.float32
BF16 = jnp.bfloat16


def _vmem_limit(block_bytes):
    need = block_bytes + (12 << 20)
    return min(need, V7X_VMEM_BYTES - (2 << 20))


def _rms(x, g):
    return x * lax.rsqrt(jnp.mean(x * x, axis=-1, keepdims=True) + EPS) * g


def _ln(x, g, b):
    mu = jnp.mean(x, axis=-1, keepdims=True)
    xc = x - mu
    var = jnp.mean(xc * xc, axis=-1, keepdims=True)
    return xc * lax.rsqrt(var + EPS) * g + b


def _gelu(x):
    inner = x * (GELU_C + (GELU_C * GELU_CUBIC) * (x * x))
    hx = 0.5 * x
    return hx + hx * jnp.tanh(inner)


def _dot(a, b):
    return jnp.dot(a, b, preferred_element_type=F32)


class _Layer(NamedTuple):
    arr: jax.Array
    layer: int


def _shape(a):
    return a.arr.shape[1:] if isinstance(a, _Layer) else a.shape


def _arr(a):
    return a.arr if isinstance(a, _Layer) else a


def _nbytes(shape, dtype):
    n = 1
    for s in shape:
        n *= s
    return n * jnp.dtype(dtype).itemsize


def _bytes(a):
    return _nbytes(_shape(a), _arr(a).dtype)


def _resident(a):
    shape = _shape(a)
    zeros = (0,) * len(shape)
    if isinstance(a, _Layer):
        return pl.BlockSpec((None,) + shape, lambda *_: (a.layer,) + zeros,
                            pipeline_mode=pl.Buffered(1))
    return pl.BlockSpec(shape, lambda *_: zeros, pipeline_mode=pl.Buffered(1))


def _whole_out(shape):
    zeros = (0,) * len(shape)
    return pl.BlockSpec(shape, lambda *_: zeros)


def _rows(a, tile, n_tiles):
    cols = _shape(a)[1]
    last = n_tiles - 1
    if isinstance(a, _Layer):
        return pl.BlockSpec((None, tile, cols), lambda i: (a.layer, jnp.minimum(i, last), 0))
    return pl.BlockSpec((tile, cols), lambda i: (jnp.minimum(i, last), 0))


def _slab_store(ref, row0, val):
    for j in range(ref.shape[0]):
        ref[j, row0:row0 + val.shape[0], :] = val[:, j * V7X_LANES:(j + 1) * V7X_LANES]


def _slab_load(ref, row0, rows):
    return jnp.concatenate([ref[j, row0:row0 + rows, :] for j in range(ref.shape[0])], axis=1)


def _causal_conv(ref, w_ref, row0, rows):
    out = []
    for j in range(ref.shape[0]):
        lanes = slice(j * V7X_LANES, (j + 1) * V7X_LANES)
        acc = w_ref[0:1, lanes] * ref[j, row0:row0 + rows, :]
        for k in range(1, w_ref.shape[0]):
            acc = acc + w_ref[k:k + 1, lanes] * ref[j, row0 + k:row0 + k + rows, :]
        out.append(acc)
    return jnp.concatenate(out, axis=1)


def _ffn_rows(x_ref, p_ref, w, o_ref, act_ref):
    pre_ref, post_ref, wg_ref, wu_ref, wd_ref = w[:5]
    d_ff = wg_ref.shape[1]
    n = x_ref.shape[0]
    sub = min(n, FFN_SUB_ROWS)
    subs = [slice(r0, r0 + sub) for r0 in range(0, n, sub)]
    xs = [x_ref[r, :] for r in subs]
    hs = [_rms(x, pre_ref[...]).astype(BF16) for x in xs]
    for r, h in zip(subs, hs):
        for c0 in range(0, d_ff, FF_COLS):
            g = _dot(h, wg_ref[:, c0:c0 + FF_COLS])
            u = _dot(h, wu_ref[:, c0:c0 + FF_COLS])
            act_ref[r, c0:c0 + FF_COLS] = (jax.nn.silu(g) * u).astype(BF16)
    ys = [_dot(act_ref[r, :], wd_ref[...]) for r in subs]
    xs = [x + 0.5 * _rms(y, post_ref[...]) for x, y in zip(xs, ys)]
    if p_ref is not None:
        epre_ref, epost_ref, ewg_ref, ewp_ref = w[5:]
        hs = [_rms(x, epre_ref[...]).astype(BF16) for x in xs]
        gates = [jax.nn.sigmoid(_dot(h, ewg_ref[...])) for h in hs]
        pes = [_dot(p_ref[r, :].astype(BF16), ewp_ref[...]) for r in subs]
        xs = [x + _rms(gate * pe, epost_ref[...]) for x, gate, pe in zip(xs, gates, pes)]
    for r, x in zip(subs, xs):
        o_ref[r, :] = x


def _ffn_body(*refs, with_embed, n_prompt):
    if with_embed:
        x_ref, xs_ref, p_ref, ps_ref = refs[:4]
        w = refs[4:-3]
    else:
        x_ref, xs_ref = refs[:2]
        p_ref = ps_ref = None
        w = refs[2:-3]
    o_ref, os_ref, act_ref = refs[-3:]
    i = pl.program_id(0)

    @pl.when(i < n_prompt)
    def _():
        _ffn_rows(x_ref, p_ref, w, o_ref, act_ref)

    @pl.when(i == n_prompt)
    def _():
        _ffn_rows(xs_ref, ps_ref, w, os_ref, act_ref)


def _ffn(x, xs, ffn_w, tile, p=None, ps=None, embed_w=None):
    n, d = x.shape
    ns = xs.shape[0]
    d_ff = _shape(ffn_w[2])[1]
    assert n % tile == 0 and ns <= tile and d_ff % FF_COLS == 0
    n_prompt = n // tile
    with_embed = embed_w is not None
    weights = list(ffn_w) + (list(embed_w) if with_embed else [])
    row = _rows(x, tile, n_prompt)
    args = [x, xs]
    in_specs = [row, _resident(xs)]
    blocks = 4 * _nbytes((tile, d), F32) + 3 * _bytes(xs) + _nbytes((tile, d_ff), BF16)
    if with_embed:
        args += [p, ps]
        in_specs += [_rows(p, tile, n_prompt), _resident(ps)]
        blocks += 2 * _nbytes((tile, _shape(p)[1]), F32) + _bytes(ps)
    blocks += sum(_bytes(a) for a in weights)
    return pl.pallas_call(
        functools.partial(_ffn_body, with_embed=with_embed, n_prompt=n_prompt),
        grid=(n_prompt + 1,),
        in_specs=in_specs + [_resident(a) for a in weights],
        out_specs=[row, _whole_out((ns, d))],
        out_shape=[jax.ShapeDtypeStruct((n, d), F32), jax.ShapeDtypeStruct((ns, d), F32)],
        scratch_shapes=[pltpu.VMEM((tile, d_ff), BF16)],
        compiler_params=pltpu.CompilerParams(
            dimension_semantics=("arbitrary",), vmem_limit_bytes=_vmem_limit(blocks)),
        name="ffn_embed" if with_embed else "ffn",
    )(*[_arr(a) for a in args + weights])


def _mixer_prompt_rows(t, x_ref, pre_ref, post_ref, win_ref, wout_ref, acw_ref, lng_ref, lnb_ref,
                       ws_ref, bias_ref, ccw_ref, ccb_ref, clg_ref, clb_ref,
                       o_ref, na_ref, nc_ref, fa_ref, fc_ref, mix_ref):
    tt = x_ref.shape[0]
    aw = acw_ref.shape[1]
    bw = ws_ref.shape[0] * ws_ref.shape[1]
    hd = ws_ref.shape[1]
    cw = ccw_ref.shape[1]

    @pl.when(t == 0)
    def _():
        fa_ref[:, 0:A_HIST_PAD, :] = jnp.zeros((fa_ref.shape[0], A_HIST_PAD, V7X_LANES), F32)
        fc_ref[:, 0:C_HIST_PAD, :] = jnp.zeros((fc_ref.shape[0], C_HIST_PAD, V7X_LANES), F32)

    x = x_ref[...]
    h = _rms(x, pre_ref[...]).astype(BF16)

    o_b = 3 * aw
    o_c = o_b + 2 * bw
    zc = _dot(h, win_ref[:, o_c:o_c + 2 * cw])
    _slab_store(fc_ref, C_HIST_PAD, zc[:, 0:cw] * jax.nn.sigmoid(zc[:, cw:2 * cw]))
    base = C_HIST_PAD - (C_KERNEL - 1)
    for r0 in range(0, tt, CONV_ROWS):
        acc = _causal_conv(fc_ref, ccw_ref, base + r0, CONV_ROWS) + ccb_ref[...]
        y_c = jax.nn.silu(_ln(acc, clg_ref[...], clb_ref[...]))
        mix_ref[r0:r0 + CONV_ROWS, aw + bw:aw + bw + cw] = y_c.astype(BF16)
    last_c = _slab_load(fc_ref, base + tt, C_KERNEL - 1)
    nc_ref[...] = last_c
    _slab_store(fc_ref, base, last_c)

    za = _dot(h, win_ref[:, 0:3 * aw])
    _slab_store(fa_ref, A_HIST_PAD, za[:, aw:2 * aw] * za[:, 0:aw])
    base = A_HIST_PAD - (A_KERNEL - 1)
    for r0 in range(0, tt, CONV_ROWS):
        y_a = za[r0:r0 + CONV_ROWS, 2 * aw:3 * aw] * _causal_conv(fa_ref, acw_ref, base + r0, CONV_ROWS)
        mix_ref[r0:r0 + CONV_ROWS, 0:aw] = y_a.astype(BF16)
    last_a = _slab_load(fa_ref, base + tt, A_KERNEL - 1)
    na_ref[...] = last_a
    _slab_store(fa_ref, base, last_a)

    u = _gelu(_dot(h, win_ref[:, o_b:o_b + bw]))
    gv = _gelu(_dot(h, win_ref[:, o_b + bw:o_b + 2 * bw]))
    row_i = lax.broadcasted_iota(jnp.int32, (CHUNK, CHUNK), 0)
    col_j = lax.broadcasted_iota(jnp.int32, (CHUNK, CHUNK), 1)
    n_chunks = tt // CHUNK
    for hh in range(B_HEADS):
        cols = slice(hh * hd, (hh + 1) * hd)
        v = _ln(gv[:, cols], lng_ref[hh:hh + 1, :], lnb_ref[hh:hh + 1, :]).astype(BF16)
        w_tril = jnp.where(col_j <= row_i, ws_ref[hh], 0.0).astype(BF16)
        bias = bias_ref[:, hh:hh + 1]
        v_cat = jnp.concatenate([v[c * CHUNK:(c + 1) * CHUNK, :] for c in range(n_chunks)], axis=1)
        sp = _dot(w_tril, v_cat)
        for c in range(n_chunks):
            rows = slice(c * CHUNK, (c + 1) * CHUNK)
            y_b = u[rows, cols] * (sp[:, c * hd:(c + 1) * hd] + bias)
            mix_ref[rows, aw + hh * hd:aw + (hh + 1) * hd] = y_b.astype(BF16)

    mix = _dot(mix_ref[...], wout_ref[...])
    o_ref[...] = x + _rms(mix, post_ref[...])


def _state_conv(state_ref, w_ref, new):
    k1 = state_ref.shape[1]
    return jnp.sum(state_ref[...] * w_ref[0:k1, :][None], axis=1) + w_ref[k1:k1 + 1, :] * new


def _state_shift(out_ref, state_ref, new):
    k1 = state_ref.shape[1]
    out_ref[:, 0:k1 - 1, :] = state_ref[:, 1:k1, :]
    out_ref[:, k1 - 1:k1, :] = new[:, None, :]


def _mixer_sample_rows(x_ref, sa_ref, sc_ref, pre_ref, post_ref, win_ref, wout_ref, acw_ref,
                       lng_ref, lnb_ref, w00_ref, b0_ref, ccw_ref, ccb_ref, clg_ref, clb_ref,
                       o_ref, na_ref, nc_ref, v_ref, mix_ref):
    n = x_ref.shape[0]
    aw = acw_ref.shape[1]
    bw = w00_ref.shape[1]
    hd = bw // B_HEADS
    cw = ccw_ref.shape[1]
    rows = slice(0, n)
    x = x_ref[...]
    h = _rms(x, pre_ref[...]).astype(BF16)

    za = _dot(h, win_ref[:, 0:3 * aw])
    fa_new = za[:, aw:2 * aw] * za[:, 0:aw]
    conv_a = _state_conv(sa_ref, acw_ref, fa_new)
    mix_ref[rows, 0:aw] = (za[:, 2 * aw:3 * aw] * conv_a).astype(BF16)
    _state_shift(na_ref, sa_ref, fa_new)

    o_b = 3 * aw
    u = _gelu(_dot(h, win_ref[:, o_b:o_b + bw]))
    gv = _gelu(_dot(h, win_ref[:, o_b + bw:o_b + 2 * bw]))
    for hh in range(B_HEADS):
        cols = slice(hh * hd, (hh + 1) * hd)
        v = _ln(gv[:, cols], lng_ref[hh:hh + 1, :], lnb_ref[hh:hh + 1, :])
        v_ref[:, cols] = v
        sp = w00_ref[:, cols] * v + b0_ref[:, cols]
        mix_ref[rows, aw + hh * hd:aw + (hh + 1) * hd] = (u[:, cols] * sp).astype(BF16)

    o_c = o_b + 2 * bw
    zc = _dot(h, win_ref[:, o_c:o_c + 2 * cw])
    fc_new = zc[:, 0:cw] * jax.nn.sigmoid(zc[:, cw:2 * cw])
    acc = _state_conv(sc_ref, ccw_ref, fc_new) + ccb_ref[...]
    y_c = jax.nn.silu(_ln(acc, clg_ref[...], clb_ref[...]))
    mix_ref[rows, aw + bw:aw + bw + cw] = y_c.astype(BF16)
    _state_shift(nc_ref, sc_ref, fc_new)

    mix = _dot(mix_ref[rows, :], wout_ref[...])
    o_ref[...] = x + _rms(mix, post_ref[...])


def _mixer_body(x_ref, xs_ref, sa_ref, sc_ref, pre_ref, post_ref, win32_ref, wout32_ref,
                acw_ref, lng_ref, lnb_ref, ws_ref, bias_ref, w00_ref, b0_ref,
                ccw_ref, ccb_ref, clg_ref, clb_ref,
                o_ref, nap_ref, ncp_ref, os_ref, nas_ref, ncs_ref, v_ref,
                fa_ref, fc_ref, mix_ref, win_ref, wout_ref, *, n_prompt, tiles_per_seq):
    s = pl.program_id(0)

    @pl.when(s == 0)
    def _():
        win_ref[...] = win32_ref[...].astype(BF16)
        wout_ref[...] = wout32_ref[...].astype(BF16)

    @pl.when(s < n_prompt)
    def _():
        _mixer_prompt_rows(lax.rem(s, tiles_per_seq), x_ref, pre_ref, post_ref, win_ref, wout_ref,
                           acw_ref, lng_ref, lnb_ref, ws_ref, bias_ref, ccw_ref, ccb_ref,
                           clg_ref, clb_ref, o_ref, nap_ref, ncp_ref, fa_ref, fc_ref, mix_ref)

    @pl.when(s == n_prompt)
    def _():
        _mixer_sample_rows(xs_ref, sa_ref, sc_ref, pre_ref, post_ref, win_ref, wout_ref, acw_ref,
                           lng_ref, lnb_ref, w00_ref, b0_ref, ccw_ref, ccb_ref, clg_ref, clb_ref,
                           os_ref, nas_ref, ncs_ref, v_ref, mix_ref)


def _mixer(x, xs, sa, sc, pre, post, w_in, w_out, a_cw, lng, lnb, ws, bias_t, w00, b0,
           c_cw, c_cb, clg, clb, tile):
    b, s, d = x.shape
    ns = xs.shape[0]
    d_in = _shape(w_in)[1]
    d_mix = _shape(w_out)[0]
    aw, cw = a_cw.shape[1], c_cw.shape[1]
    bw = w00.shape[1]
    assert s % tile == 0 and tile % CHUNK == 0 and tile % CONV_ROWS == 0 and ns <= tile
    assert ws.shape[1] == CHUNK and ws.shape[2] == CHUNK
    assert aw % V7X_LANES == 0 and cw % V7X_LANES == 0
    tiles_per_seq = s // tile
    n_prompt = b * tiles_per_seq
    last = n_prompt - 1

    def seq_of(i):
        return lax.div(jnp.minimum(i, last), tiles_per_seq)

    def tile_of(i):
        return lax.rem(jnp.minimum(i, last), tiles_per_seq)

    xblk = pl.BlockSpec((None, tile, d), lambda i: (seq_of(i), tile_of(i), 0))
    args = [x, xs, sa, sc, pre, post, w_in, w_out, a_cw, lng, lnb, ws, bias_t, w00, b0,
            c_cw, c_cb, clg, clb]
    sample_out = [(ns, d), _shape(sa), _shape(sc), (ns, bw)]
    blocks = (4 * _nbytes((tile, d), F32) + sum(_bytes(a) for a in args[1:])
              + 2 * sum(_nbytes(o, F32) for o in sample_out)
              + _nbytes((d, d_in), BF16) + _nbytes((d_mix, d), BF16)
              + _nbytes((tile, d_mix), BF16) + 2 * _nbytes((tile, d_in), F32))
    return pl.pallas_call(
        functools.partial(_mixer_body, n_prompt=n_prompt, tiles_per_seq=tiles_per_seq),
        grid=(n_prompt + 1,),
        in_specs=[xblk] + [_resident(a) for a in args[1:]],
        out_specs=[xblk,
                   pl.BlockSpec((None, A_KERNEL - 1, aw), lambda i: (seq_of(i), 0, 0)),
                   pl.BlockSpec((None, C_KERNEL - 1, cw), lambda i: (seq_of(i), 0, 0))]
                  + [_whole_out(o) for o in sample_out],
        out_shape=[jax.ShapeDtypeStruct((b, s, d), F32),
                   jax.ShapeDtypeStruct((b, A_KERNEL - 1, aw), F32),
                   jax.ShapeDtypeStruct((b, C_KERNEL - 1, cw), F32)]
                  + [jax.ShapeDtypeStruct(o, F32) for o in sample_out],
        scratch_shapes=[pltpu.VMEM((aw // V7X_LANES, A_HIST_PAD + tile, V7X_LANES), F32),
                        pltpu.VMEM((cw // V7X_LANES, C_HIST_PAD + tile, V7X_LANES), F32),
                        pltpu.VMEM((tile, d_mix), BF16),
                        pltpu.VMEM((d, d_in), BF16), pltpu.VMEM((d_mix, d), BF16)],
        compiler_params=pltpu.CompilerParams(
            dimension_semantics=("arbitrary",), vmem_limit_bytes=_vmem_limit(blocks)),
        name="mixer",
    )(*[_arr(a) for a in args])


def kernel(x_prompt, x_sample, state_conv_a, state_conv_c, p_prompt, p_sample, f1_pre, f1_post, f1_wg, f1_wu, f1_wd, m_pre, m_post, w_in, w_out, a_conv_w, b_ln_g, b_ln_b, b_ws, b_bias, c_conv_w, c_conv_b, c_ln_g, c_ln_b, f2_pre, f2_post, f2_wg, f2_wu, f2_wd, e_pre, e_post, e_wg, e_wp):
    depth = f1_wg.shape[0]
    nb, seq, d = x_prompt.shape
    ns = x_sample.shape[0]
    assert x_sample.shape[1] == 1
    hd = b_ws.shape[2]
    bw = B_HEADS * hd

    yp = x_prompt.reshape(nb * seq, d)
    ys = x_sample.reshape(ns, d)
    pp = p_prompt.reshape(depth, nb * seq, -1)
    ps = p_sample.reshape(depth, ns, -1)
    a_p, c_p, a_s, c_s, v_s = [], [], [], [], []
    for i in range(depth):
        row = lambda g: g[i].reshape(1, -1)
        w = lambda a: _Layer(a, i)
        f1 = (row(f1_pre), row(f1_post), w(f1_wg), w(f1_wu), w(f1_wd))
        f2 = (row(f2_pre), row(f2_post), w(f2_wg), w(f2_wu), w(f2_wd))
        em = (row(e_pre), row(e_post), w(e_wg), w(e_wp))
        w00 = jnp.repeat(b_ws[i, :, 0, 0], hd).reshape(1, bw)
        b0 = jnp.repeat(b_bias[i, :, 0], hd).reshape(1, bw)

        yp, ys = _ffn(yp, ys, f1, tile=PROMPT_TILE)
        yp, na_p, nc_p, ys, na_s, nc_s, vr_s = _mixer(
            yp.reshape(nb, seq, d), ys, w(state_conv_a), w(state_conv_c),
            row(m_pre), row(m_post), w(w_in), w(w_out),
            a_conv_w[i], b_ln_g[i], b_ln_b[i], b_ws[i], b_bias[i].T, w00, b0,
            c_conv_w[i], row(c_conv_b), row(c_ln_g), row(c_ln_b), tile=PROMPT_TILE)
        yp, ys = _ffn(yp.reshape(nb * seq, d), ys, f2, tile=PROMPT_TILE,
                      p=_Layer(pp, i), ps=_Layer(ps, i), embed_w=em)

        a_p.append(na_p)
        c_p.append(nc_p)
        a_s.append(na_s)
        c_s.append(nc_s)
        v_s.append(vr_s.reshape(ns, 1, bw))
    return (yp.reshape(nb, seq, d), ys.reshape(ns, 1, d), jnp.stack(a_p), jnp.stack(c_p),
            jnp.stack(a_s), jnp.stack(c_s), jnp.stack(v_s))
```

```python
import functools
import math
from typing import NamedTuple

import jax
import jax.numpy as jnp
from jax import lax
from jax.experimental import pallas as pl
from jax.experimental.pallas import tpu as pltpu

EPS = 1e-6
CHUNK = 128
A_KERNEL = 3
C_KERNEL = 31
B_HEADS = 4

V7X_SUBLANES = 8
V7X_LANES = 128
V7X_MXU_COLS = 256
V7X_VMEM_BYTES = 64 * 1024 * 1024

PROMPT_TILE = 512
FFN_SUB_ROWS = 256
FF_COLS = V7X_MXU_COLS
CONV_ROWS = 32
A_HIST_PAD = V7X_SUBLANES
C_HIST_PAD = 4 * V7X_SUBLANES

GELU_C = math.sqrt(2.0 / math.pi)
GELU_CUBIC = 0.044715

F32 = jnp.float32
BF16 = jnp.bfloat16


def _vmem_limit(block_bytes):
    need = block_bytes + (12 << 20)
    return min(need, V7X_VMEM_BYTES - (2 << 20))


def _rms(x, g):
    return x * lax.rsqrt(jnp.mean(x * x, axis=-1, keepdims=True) + EPS) * g


def _ln(x, g, b):
    mu = jnp.mean(x, axis=-1, keepdims=True)
    xc = x - mu
    var = jnp.mean(xc * xc, axis=-1, keepdims=True)
    return xc * lax.rsqrt(var + EPS) * g + b


def _gelu(x):
    inner = x * (GELU_C + (GELU_C * GELU_CUBIC) * (x * x))
    hx = 0.5 * x
    return hx + hx * jnp.tanh(inner)


def _dot(a, b):
    return jnp.dot(a, b, preferred_element_type=F32)


class _Layer(NamedTuple):
    arr: jax.Array
    layer: int


def _shape(a):
    return a.arr.shape[1:] if isinstance(a, _Layer) else a.shape


def _arr(a):
    return a.arr if isinstance(a, _Layer) else a


def _nbytes(shape, dtype):
    n = 1
    for s in shape:
        n *= s
    return n * jnp.dtype(dtype).itemsize


def _bytes(a):
    return _nbytes(_shape(a), _arr(a).dtype)


def _resident(a):
    shape = _shape(a)
    zeros = (0,) * len(shape)
    if isinstance(a, _Layer):
        return pl.BlockSpec((None,) + shape, lambda *_: (a.layer,) + zeros,
                            pipeline_mode=pl.Buffered(1))
    return pl.BlockSpec(shape, lambda *_: zeros, pipeline_mode=pl.Buffered(1))


def _whole_out(shape):
    zeros = (0,) * len(shape)
    return pl.BlockSpec(shape, lambda *_: zeros)


def _rows(a, tile, n_tiles):
    cols = _shape(a)[1]
    last = n_tiles - 1
    if isinstance(a, _Layer):
        return pl.BlockSpec((None, tile, cols), lambda i: (a.layer, jnp.minimum(i, last), 0))
    return pl.BlockSpec((tile, cols), lambda i: (jnp.minimum(i, last), 0))


def _slab_store(ref, row0, val):
    for j in range(ref.shape[0]):
        ref[j, row0:row0 + val.shape[0], :] = val[:, j * V7X_LANES:(j + 1) * V7X_LANES]


def _slab_load(ref, row0, rows):
    return jnp.concatenate([ref[j, row0:row0 + rows, :] for j in range(ref.shape[0])], axis=1)


def _causal_conv(ref, w_ref, row0, rows):
    out = []
    for j in range(ref.shape[0]):
        lanes = slice(j * V7X_LANES, (j + 1) * V7X_LANES)
        acc = w_ref[0:1, lanes] * ref[j, row0:row0 + rows, :]
        for k in range(1, w_ref.shape[0]):
            acc = acc + w_ref[k:k + 1, lanes] * ref[j, row0 + k:row0 + k + rows, :]
        out.append(acc)
    return jnp.concatenate(out, axis=1)


def _ffn_rows(x_ref, p_ref, w, o_ref, act_ref):
    pre_ref, post_ref, wg_ref, wu_ref, wd_ref = w[:5]
    d_ff = wg_ref.shape[1]
    n = x_ref.shape[0]
    sub = min(n, FFN_SUB_ROWS)
    subs = [slice(r0, r0 + sub) for r0 in range(0, n, sub)]
    xs = [x_ref[r, :] for r in subs]
    hs = [_rms(x, pre_ref[...]).astype(BF16) for x in xs]
    for r, h in zip(subs, hs):
        for c0 in range(0, d_ff, FF_COLS):
            g = _dot(h, wg_ref[:, c0:c0 + FF_COLS])
            u = _dot(h, wu_ref[:, c0:c0 + FF_COLS])
            act_ref[r, c0:c0 + FF_COLS] = (jax.nn.silu(g) * u).astype(BF16)
    ys = [_dot(act_ref[r, :], wd_ref[...]) for r in subs]
    xs = [x + 0.5 * _rms(y, post_ref[...]) for x, y in zip(xs, ys)]
    if p_ref is not None:
        epre_ref, epost_ref, ewg_ref, ewp_ref = w[5:]
        hs = [_rms(x, epre_ref[...]).astype(BF16) for x in xs]
        gates = [jax.nn.sigmoid(_dot(h, ewg_ref[...])) for h in hs]
        pes = [_dot(p_ref[r, :].astype(BF16), ewp_ref[...]) for r in subs]
        xs = [x + _rms(gate * pe, epost_ref[...]) for x, gate, pe in zip(xs, gates, pes)]
    for r, x in zip(subs, xs):
        o_ref[r, :] = x


def _ffn_body(*refs, with_embed, n_prompt):
    if with_embed:
        x_ref, xs_ref, p_ref, ps_ref = refs[:4]
        w = refs[4:-3]
    else:
        x_ref, xs_ref = refs[:2]
        p_ref = ps_ref = None
        w = refs[2:-3]
    o_ref, os_ref, act_ref = refs[-3:]
    i = pl.program_id(0)

    @pl.when(i < n_prompt)
    def _():
        _ffn_rows(x_ref, p_ref, w, o_ref, act_ref)

    @pl.when(i == n_prompt)
    def _():
        _ffn_rows(xs_ref, ps_ref, w, os_ref, act_ref)


def _ffn(x, xs, ffn_w, tile, p=None, ps=None, embed_w=None):
    n, d = x.shape
    ns = xs.shape[0]
    d_ff = _shape(ffn_w[2])[1]
    assert n % tile == 0 and ns <= tile and d_ff % FF_COLS == 0
    n_prompt = n // tile
    with_embed = embed_w is not None
    weights = list(ffn_w) + (list(embed_w) if with_embed else [])
    row = _rows(x, tile, n_prompt)
    args = [x, xs]
    in_specs = [row, _resident(xs)]
    blocks = 4 * _nbytes((tile, d), F32) + 3 * _bytes(xs) + _nbytes((tile, d_ff), BF16)
    if with_embed:
        args += [p, ps]
        in_specs += [_rows(p, tile, n_prompt), _resident(ps)]
        blocks += 2 * _nbytes((tile, _shape(p)[1]), F32) + _bytes(ps)
    blocks += sum(_bytes(a) for a in weights)
    return pl.pallas_call(
        functools.partial(_ffn_body, with_embed=with_embed, n_prompt=n_prompt),
        grid=(n_prompt + 1,),
        in_specs=in_specs + [_resident(a) for a in weights],
        out_specs=[row, _whole_out((ns, d))],
        out_shape=[jax.ShapeDtypeStruct((n, d), F32), jax.ShapeDtypeStruct((ns, d), F32)],
        scratch_shapes=[pltpu.VMEM((tile, d_ff), BF16)],
        compiler_params=pltpu.CompilerParams(
            dimension_semantics=("arbitrary",), vmem_limit_bytes=_vmem_limit(blocks)),
        name="ffn_embed" if with_embed else "ffn",
    )(*[_arr(a) for a in args + weights])


def _mixer_prompt_rows(t, x_ref, pre_ref, post_ref, win_ref, wout_ref, acw_ref, lng_ref, lnb_ref,
                       ws_ref, bias_ref, ccw_ref, ccb_ref, clg_ref, clb_ref,
                       o_ref, na_ref, nc_ref, fa_ref, fc_ref, mix_ref):
    tt = x_ref.shape[0]
    aw = acw_ref.shape[1]
    bw = ws_ref.shape[0] * ws_ref.shape[1]
    hd = ws_ref.shape[1]
    cw = ccw_ref.shape[1]

    @pl.when(t == 0)
    def _():
        fa_ref[:, 0:A_HIST_PAD, :] = jnp.zeros((fa_ref.shape[0], A_HIST_PAD, V7X_LANES), F32)
        fc_ref[:, 0:C_HIST_PAD, :] = jnp.zeros((fc_ref.shape[0], C_HIST_PAD, V7X_LANES), F32)

    x = x_ref[...]
    h = _rms(x, pre_ref[...]).astype(BF16)

    o_b = 3 * aw
    o_c = o_b + 2 * bw
    zc = _dot(h, win_ref[:, o_c:o_c + 2 * cw])
    _slab_store(fc_ref, C_HIST_PAD, zc[:, 0:cw] * jax.nn.sigmoid(zc[:, cw:2 * cw]))
    base = C_HIST_PAD - (C_KERNEL - 1)
    for r0 in range(0, tt, CONV_ROWS):
        acc = _causal_conv(fc_ref, ccw_ref, base + r0, CONV_ROWS) + ccb_ref[...]
        y_c = jax.nn.silu(_ln(acc, clg_ref[...], clb_ref[...]))
        mix_ref[r0:r0 + CONV_ROWS, aw + bw:aw + bw + cw] = y_c.astype(BF16)
    last_c = _slab_load(fc_ref, base + tt, C_KERNEL - 1)
    nc_ref[...] = last_c
    _slab_store(fc_ref, base, last_c)

    za = _dot(h, win_ref[:, 0:3 * aw])
    _slab_store(fa_ref, A_HIST_PAD, za[:, aw:2 * aw] * za[:, 0:aw])
    base = A_HIST_PAD - (A_KERNEL - 1)
    for r0 in range(0, tt, CONV_ROWS):
        y_a = za[r0:r0 + CONV_ROWS, 2 * aw:3 * aw] * _causal_conv(fa_ref, acw_ref, base + r0, CONV_ROWS)
        mix_ref[r0:r0 + CONV_ROWS, 0:aw] = y_a.astype(BF16)
    last_a = _slab_load(fa_ref, base + tt, A_KERNEL - 1)
    na_ref[...] = last_a
    _slab_store(fa_ref, base, last_a)

    u = _gelu(_dot(h, win_ref[:, o_b:o_b + bw]))
    gv = _gelu(_dot(h, win_ref[:, o_b + bw:o_b + 2 * bw]))
    row_i = lax.broadcasted_iota(jnp.int32, (CHUNK, CHUNK), 0)
    col_j = lax.broadcasted_iota(jnp.int32, (CHUNK, CHUNK), 1)
    n_chunks = tt // CHUNK
    for hh in range(B_HEADS):
        cols = slice(hh * hd, (hh + 1) * hd)
        v = _ln(gv[:, cols], lng_ref[hh:hh + 1, :], lnb_ref[hh:hh + 1, :]).astype(BF16)
        w_tril = jnp.where(col_j <= row_i, ws_ref[hh], 0.0).astype(BF16)
        bias = bias_ref[:, hh:hh + 1]
        v_cat = jnp.concatenate([v[c * CHUNK:(c + 1) * CHUNK, :] for c in range(n_chunks)], axis=1)
        sp = _dot(w_tril, v_cat)
        for c in range(n_chunks):
            rows = slice(c * CHUNK, (c + 1) * CHUNK)
            y_b = u[rows, cols] * (sp[:, c * hd:(c + 1) * hd] + bias)
            mix_ref[rows, aw + hh * hd:aw + (hh + 1) * hd] = y_b.astype(BF16)

    mix = _dot(mix_ref[...], wout_ref[...])
    o_ref[...] = x + _rms(mix, post_ref[...])


def _state_conv(state_ref, w_ref, new):
    k1 = state_ref.shape[0]
    acc = w_ref[k1:k1 + 1, :] * new
    for k in range(k1):
        acc = acc + w_ref[k:k + 1, :] * state_ref[k]
    return acc


def _state_shift(out_ref, state_ref, new):
    k1 = state_ref.shape[0]
    out_ref[0:k1 - 1] = state_ref[1:k1]
    out_ref[k1 - 1] = new


def _mixer_sample_rows(x_ref, sa_ref, sc_ref, pre_ref, post_ref, win_ref, wout_ref, acw_ref,
                       lng_ref, lnb_ref, w00_ref, b0_ref, ccw_ref, ccb_ref, clg_ref, clb_ref,
                       o_ref, na_ref, nc_ref, v_ref, mix_ref):
    n = x_ref.shape[0]
    aw = acw_ref.shape[1]
    bw = w00_ref.shape[1]
    hd = bw // B_HEADS
    cw = ccw_ref.shape[1]
    rows = slice(0, n)
    x = x_ref[...]
    h = _rms(x, pre_ref[...]).astype(BF16)

    za = _dot(h, win_ref[:, 0:3 * aw])
    fa_new = za[:, aw:2 * aw] * za[:, 0:aw]
    conv_a = _state_conv(sa_ref, acw_ref, fa_new)
    mix_ref[rows, 0:aw] = (za[:, 2 * aw:3 * aw] * conv_a).astype(BF16)
    _state_shift(na_ref, sa_ref, fa_new)

    o_b = 3 * aw
    u = _gelu(_dot(h, win_ref[:, o_b:o_b + bw]))
    gv = _gelu(_dot(h, win_ref[:, o_b + bw:o_b + 2 * bw]))
    for hh in range(B_HEADS):
        cols = slice(hh * hd, (hh + 1) * hd)
        v = _ln(gv[:, cols], lng_ref[hh:hh + 1, :], lnb_ref[hh:hh + 1, :])
        v_ref[:, cols] = v
        sp = w00_ref[:, cols] * v + b0_ref[:, cols]
        mix_ref[rows, aw + hh * hd:aw + (hh + 1) * hd] = (u[:, cols] * sp).astype(BF16)

    o_c = o_b + 2 * bw
    zc = _dot(h, win_ref[:, o_c:o_c + 2 * cw])
    fc_new = zc[:, 0:cw] * jax.nn.sigmoid(zc[:, cw:2 * cw])
    acc = _state_conv(sc_ref, ccw_ref, fc_new) + ccb_ref[...]
    y_c = jax.nn.silu(_ln(acc, clg_ref[...], clb_ref[...]))
    mix_ref[rows, aw + bw:aw + bw + cw] = y_c.astype(BF16)
    _state_shift(nc_ref, sc_ref, fc_new)

    mix = _dot(mix_ref[rows, :], wout_ref[...])
    o_ref[...] = x + _rms(mix, post_ref[...])


def _mixer_body(x_ref, xs_ref, sa_ref, sc_ref, pre_ref, post_ref, win32_ref, wout32_ref,
                acw_ref, lng_ref, lnb_ref, ws_ref, bias_ref, w00_ref, b0_ref,
                ccw_ref, ccb_ref, clg_ref, clb_ref,
                o_ref, nap_ref, ncp_ref, os_ref, nas_ref, ncs_ref, v_ref,
                fa_ref, fc_ref, mix_ref, win_ref, wout_ref, *, n_prompt, tiles_per_seq):
    s = pl.program_id(0)

    @pl.when(s == 0)
    def _():
        win_ref[...] = win32_ref[...].astype(BF16)
        wout_ref[...] = wout32_ref[...].astype(BF16)

    @pl.when(s < n_prompt)
    def _():
        _mixer_prompt_rows(lax.rem(s, tiles_per_seq), x_ref, pre_ref, post_ref, win_ref, wout_ref,
                           acw_ref, lng_ref, lnb_ref, ws_ref, bias_ref, ccw_ref, ccb_ref,
                           clg_ref, clb_ref, o_ref, nap_ref, ncp_ref, fa_ref, fc_ref, mix_ref)

    @pl.when(s == n_prompt)
    def _():
        _mixer_sample_rows(xs_ref, sa_ref, sc_ref, pre_ref, post_ref, win_ref, wout_ref, acw_ref,
                           lng_ref, lnb_ref, w00_ref, b0_ref, ccw_ref, ccb_ref, clg_ref, clb_ref,
                           os_ref, nas_ref, ncs_ref, v_ref, mix_ref)


def _mixer(x, xs, sa, sc, pre, post, w_in, w_out, a_cw, lng, lnb, ws, bias_t, w00, b0,
           c_cw, c_cb, clg, clb, tile):
    b, s, d = x.shape
    ns = xs.shape[0]
    d_in = _shape(w_in)[1]
    d_mix = _shape(w_out)[0]
    aw, cw = a_cw.shape[1], c_cw.shape[1]
    bw = w00.shape[1]
    assert s % tile == 0 and tile % CHUNK == 0 and tile % CONV_ROWS == 0 and ns <= tile
    assert ws.shape[1] == CHUNK and ws.shape[2] == CHUNK
    assert aw % V7X_LANES == 0 and cw % V7X_LANES == 0
    tiles_per_seq = s // tile
    n_prompt = b * tiles_per_seq
    last = n_prompt - 1

    def seq_of(i):
        return lax.div(jnp.minimum(i, last), tiles_per_seq)

    def tile_of(i):
        return lax.rem(jnp.minimum(i, last), tiles_per_seq)

    xblk = pl.BlockSpec((None, tile, d), lambda i: (seq_of(i), tile_of(i), 0))
    args = [x, xs, sa, sc, pre, post, w_in, w_out, a_cw, lng, lnb, ws, bias_t, w00, b0,
            c_cw, c_cb, clg, clb]
    sample_out = [(ns, d), _shape(sa), _shape(sc), (ns, bw)]
    blocks = (4 * _nbytes((tile, d), F32) + sum(_bytes(a) for a in args[1:])
              + 2 * sum(_nbytes(o, F32) for o in sample_out)
              + _nbytes((d, d_in), BF16) + _nbytes((d_mix, d), BF16)
              + _nbytes((tile, d_mix), BF16) + 2 * _nbytes((tile, d_in), F32))
    return pl.pallas_call(
        functools.partial(_mixer_body, n_prompt=n_prompt, tiles_per_seq=tiles_per_seq),
        grid=(n_prompt + 1,),
        in_specs=[xblk] + [_resident(a) for a in args[1:]],
        out_specs=[xblk,
                   pl.BlockSpec((None, A_KERNEL - 1, aw), lambda i: (seq_of(i), 0, 0)),
                   pl.BlockSpec((None, C_KERNEL - 1, cw), lambda i: (seq_of(i), 0, 0))]
                  + [_whole_out(o) for o in sample_out],
        out_shape=[jax.ShapeDtypeStruct((b, s, d), F32),
                   jax.ShapeDtypeStruct((b, A_KERNEL - 1, aw), F32),
                   jax.ShapeDtypeStruct((b, C_KERNEL - 1, cw), F32)]
                  + [jax.ShapeDtypeStruct(o, F32) for o in sample_out],
        scratch_shapes=[pltpu.VMEM((aw // V7X_LANES, A_HIST_PAD + tile, V7X_LANES), F32),
                        pltpu.VMEM((cw // V7X_LANES, C_HIST_PAD + tile, V7X_LANES), F32),
                        pltpu.VMEM((tile, d_mix), BF16),
                        pltpu.VMEM((d, d_in), BF16), pltpu.VMEM((d_mix, d), BF16)],
        compiler_params=pltpu.CompilerParams(
            dimension_semantics=("arbitrary",), vmem_limit_bytes=_vmem_limit(blocks)),
        name="mixer",
    )(*[_arr(a) for a in args])


def kernel(x_prompt, x_sample, state_conv_a, state_conv_c, p_prompt, p_sample, f1_pre, f1_post, f1_wg, f1_wu, f1_wd, m_pre, m_post, w_in, w_out, a_conv_w, b_ln_g, b_ln_b, b_ws, b_bias, c_conv_w, c_conv_b, c_ln_g, c_ln_b, f2_pre, f2_post, f2_wg, f2_wu, f2_wd, e_pre, e_post, e_wg, e_wp):
    depth = f1_wg.shape[0]
    nb, seq, d = x_prompt.shape
    ns = x_sample.shape[0]
    assert x_sample.shape[1] == 1
    hd = b_ws.shape[2]
    bw = B_HEADS * hd

    yp = x_prompt.reshape(nb * seq, d)
    ys = x_sample.reshape(ns, d)
    sa_t = jnp.transpose(state_conv_a, (0, 2, 1, 3))
    sc_t = jnp.transpose(state_conv_c, (0, 2, 1, 3))
    pp = p_prompt.reshape(depth, nb * seq, -1)
    ps = p_sample.reshape(depth, ns, -1)
    a_p, c_p, a_s, c_s, v_s = [], [], [], [], []
    for i in range(depth):
        row = lambda g: g[i].reshape(1, -1)
        w = lambda a: _Layer(a, i)
        f1 = (row(f1_pre), row(f1_post), w(f1_wg), w(f1_wu), w(f1_wd))
        f2 = (row(f2_pre), row(f2_post), w(f2_wg), w(f2_wu), w(f2_wd))
        em = (row(e_pre), row(e_post), w(e_wg), w(e_wp))
        w00 = jnp.repeat(b_ws[i, :, 0, 0], hd).reshape(1, bw)
        b0 = jnp.repeat(b_bias[i, :, 0], hd).reshape(1, bw)

        yp, ys = _ffn(yp, ys, f1, tile=PROMPT_TILE)
        yp, na_p, nc_p, ys, na_s, nc_s, vr_s = _mixer(
            yp.reshape(nb, seq, d), ys, w(sa_t), w(sc_t),
            row(m_pre), row(m_post), w(w_in), w(w_out),
            a_conv_w[i], b_ln_g[i], b_ln_b[i], b_ws[i], b_bias[i].T, w00, b0,
            c_conv_w[i], row(c_conv_b), row(c_ln_g), row(c_ln_b), tile=PROMPT_TILE)
        yp, ys = _ffn(yp.reshape(nb * seq, d), ys, f2, tile=PROMPT_TILE,
                      p=_Layer(pp, i), ps=_Layer(ps, i), embed_w=em)

        a_p.append(na_p)
        c_p.append(nc_p)
        a_s.append(na_s)
        c_s.append(nc_s)
        v_s.append(vr_s.reshape(ns, 1, bw))
    return (yp.reshape(nb, seq, d), ys.reshape(ns, 1, d), jnp.stack(a_p), jnp.stack(c_p),
            jnp.transpose(jnp.stack(a_s), (0, 2, 1, 3)),
            jnp.transpose(jnp.stack(c_s), (0, 2, 1, 3)), jnp.stack(v_s))
```

```python
import functools
import math
from typing import NamedTuple

import jax
import jax.numpy as jnp
from jax import lax
from jax.experimental import pallas as pl
from jax.experimental.pallas import tpu as pltpu

EPS = 1e-6
CHUNK = 128
A_KERNEL = 3
C_KERNEL = 31
B_HEADS = 4

V7X_SUBLANES = 8
V7X_LANES = 128
V7X_MXU_COLS = 256
V7X_VMEM_BYTES = 64 * 1024 * 1024

TILE_CANDIDATES = (1024, 512, 256, 128)
VMEM_HEADROOM = 5 << 20
FFN_SUB_ROWS = 256
FF_COLS = V7X_MXU_COLS
CONV_ROWS = 32
A_HIST_PAD = V7X_SUBLANES
C_HIST_PAD = 4 * V7X_SUBLANES

GELU_C = math.sqrt(2.0 / math.pi)
GELU_CUBIC = 0.044715

F32 = jnp.float32
BF16 = jnp.bfloat16


def _vmem_limit(block_bytes):
    return min(block_bytes + 2 * VMEM_HEADROOM, V7X_VMEM_BYTES - (2 << 20))


def _pick_tile(n_rows, fixed_bytes, row_bytes, multiple_of=1):
    for tile in TILE_CANDIDATES:
        fits = fixed_bytes + tile * row_bytes <= V7X_VMEM_BYTES - VMEM_HEADROOM
        if n_rows % tile == 0 and tile % multiple_of == 0 and fits:
            return tile
    raise ValueError("no prompt tile fits VMEM")


def _rms(x, g):
    return x * lax.rsqrt(jnp.mean(x * x, axis=-1, keepdims=True) + EPS) * g


def _ln(x, g, b):
    mu = jnp.mean(x, axis=-1, keepdims=True)
    xc = x - mu
    var = jnp.mean(xc * xc, axis=-1, keepdims=True)
    return xc * lax.rsqrt(var + EPS) * g + b


def _gelu(x):
    inner = x * (GELU_C + (GELU_C * GELU_CUBIC) * (x * x))
    hx = 0.5 * x
    return hx + hx * jnp.tanh(inner)


def _dot(a, b):
    return jnp.dot(a, b, preferred_element_type=F32)


class _Layer(NamedTuple):
    arr: jax.Array
    layer: int


def _shape(a):
    return a.arr.shape[1:] if isinstance(a, _Layer) else a.shape


def _arr(a):
    return a.arr if isinstance(a, _Layer) else a


def _nbytes(shape, dtype):
    n = 1
    for s in shape:
        n *= s
    return n * jnp.dtype(dtype).itemsize


def _bytes(a):
    return _nbytes(_shape(a), _arr(a).dtype)


def _resident(a):
    shape = _shape(a)
    zeros = (0,) * len(shape)
    if isinstance(a, _Layer):
        return pl.BlockSpec((None,) + shape, lambda *_: (a.layer,) + zeros,
                            pipeline_mode=pl.Buffered(1))
    return pl.BlockSpec(shape, lambda *_: zeros, pipeline_mode=pl.Buffered(1))


def _whole_out(shape):
    zeros = (0,) * len(shape)
    return pl.BlockSpec(shape, lambda *_: zeros)


def _rows(a, tile, n_tiles):
    cols = _shape(a)[1]
    last = n_tiles - 1
    if isinstance(a, _Layer):
        return pl.BlockSpec((None, tile, cols), lambda i: (a.layer, jnp.minimum(i, last), 0))
    return pl.BlockSpec((tile, cols), lambda i: (jnp.minimum(i, last), 0))


def _slab_store(ref, row0, val):
    for j in range(ref.shape[0]):
        ref[j, row0:row0 + val.shape[0], :] = val[:, j * V7X_LANES:(j + 1) * V7X_LANES]


def _slab_load(ref, row0, rows):
    return jnp.concatenate([ref[j, row0:row0 + rows, :] for j in range(ref.shape[0])], axis=1)


def _causal_conv(ref, w_ref, row0, rows):
    out = []
    for j in range(ref.shape[0]):
        lanes = slice(j * V7X_LANES, (j + 1) * V7X_LANES)
        acc = w_ref[0:1, lanes] * ref[j, row0:row0 + rows, :]
        for k in range(1, w_ref.shape[0]):
            acc = acc + w_ref[k:k + 1, lanes] * ref[j, row0 + k:row0 + k + rows, :]
        out.append(acc)
    return jnp.concatenate(out, axis=1)


def _ffn_rows(x_ref, p_ref, w, o_ref, act_ref):
    pre_ref, post_ref, wg_ref, wu_ref, wd_ref = w[:5]
    d_ff = wg_ref.shape[1]
    n = x_ref.shape[0]
    sub = min(n, FFN_SUB_ROWS)
    subs = [slice(r0, r0 + sub) for r0 in range(0, n, sub)]
    xs = [x_ref[r, :] for r in subs]
    hs = [_rms(x, pre_ref[...]).astype(BF16) for x in xs]
    for r, h in zip(subs, hs):
        for c0 in range(0, d_ff, FF_COLS):
            g = _dot(h, wg_ref[:, c0:c0 + FF_COLS])
            u = _dot(h, wu_ref[:, c0:c0 + FF_COLS])
            act_ref[r, c0:c0 + FF_COLS] = (jax.nn.silu(g) * u).astype(BF16)
    for r in subs:
        o_ref[r, :] = _dot(act_ref[r, :], wd_ref[...])
    xs = [x_ref[r, :] + 0.5 * _rms(o_ref[r, :], post_ref[...]) for r in subs]
    if p_ref is not None:
        epre_ref, epost_ref, ewg_ref, ewp_ref = w[5:]
        hs = [_rms(x, epre_ref[...]).astype(BF16) for x in xs]
        gates = [jax.nn.sigmoid(_dot(h, ewg_ref[...])) for h in hs]
        pes = [_dot(p_ref[r, :].astype(BF16), ewp_ref[...]) for r in subs]
        xs = [x + _rms(gate * pe, epost_ref[...]) for x, gate, pe in zip(xs, gates, pes)]
    for r, x in zip(subs, xs):
        o_ref[r, :] = x


def _ffn_body(*refs, with_embed, n_prompt):
    if with_embed:
        x_ref, xs_ref, p_ref, ps_ref = refs[:4]
        w = refs[4:-3]
    else:
        x_ref, xs_ref = refs[:2]
        p_ref = ps_ref = None
        w = refs[2:-3]
    o_ref, os_ref, act_ref = refs[-3:]
    i = pl.program_id(0)

    @pl.when(i < n_prompt)
    def _():
        _ffn_rows(x_ref, p_ref, w, o_ref, act_ref)

    @pl.when(i == n_prompt)
    def _():
        _ffn_rows(xs_ref, ps_ref, w, os_ref, act_ref)


def _ffn(x, xs, ffn_w, p=None, ps=None, embed_w=None):
    n, d = x.shape
    ns = xs.shape[0]
    d_ff = _shape(ffn_w[2])[1]
    with_embed = embed_w is not None
    weights = list(ffn_w) + (list(embed_w) if with_embed else [])
    fixed = sum(_bytes(a) for a in weights) + 3 * _bytes(xs)
    per_row = 4 * _nbytes((d,), F32) + _nbytes((d_ff,), BF16)
    if with_embed:
        fixed += _bytes(ps)
        per_row += 2 * _nbytes((_shape(p)[1],), F32)
    tile = _pick_tile(n, fixed, per_row)
    assert ns <= tile and d_ff % FF_COLS == 0
    blocks = fixed + tile * per_row
    n_prompt = n // tile
    row = _rows(x, tile, n_prompt)
    args = [x, xs]
    in_specs = [row, _resident(xs)]
    if with_embed:
        args += [p, ps]
        in_specs += [_rows(p, tile, n_prompt), _resident(ps)]
    return pl.pallas_call(
        functools.partial(_ffn_body, with_embed=with_embed, n_prompt=n_prompt),
        grid=(n_prompt + 1,),
        in_specs=in_specs + [_resident(a) for a in weights],
        out_specs=[row, _whole_out((ns, d))],
        out_shape=[jax.ShapeDtypeStruct((n, d), F32), jax.ShapeDtypeStruct((ns, d), F32)],
        scratch_shapes=[pltpu.VMEM((tile, d_ff), BF16)],
        compiler_params=pltpu.CompilerParams(
            dimension_semantics=("arbitrary",), vmem_limit_bytes=_vmem_limit(blocks)),
        name="ffn_embed" if with_embed else "ffn",
    )(*[_arr(a) for a in args + weights])


def _mixer_prompt_rows(t, x_ref, pre_ref, post_ref, win_ref, wout_ref, acw_ref, lng_ref, lnb_ref,
                       ws_ref, bias_ref, ccw_ref, ccb_ref, clg_ref, clb_ref,
                       o_ref, na_ref, nc_ref, fa_ref, fc_ref, mix_ref):
    tt = x_ref.shape[0]
    aw = acw_ref.shape[1]
    bw = ws_ref.shape[0] * ws_ref.shape[1]
    hd = ws_ref.shape[1]
    cw = ccw_ref.shape[1]

    @pl.when(t == 0)
    def _():
        fa_ref[:, 0:A_HIST_PAD, :] = jnp.zeros((fa_ref.shape[0], A_HIST_PAD, V7X_LANES), F32)
        fc_ref[:, 0:C_HIST_PAD, :] = jnp.zeros((fc_ref.shape[0], C_HIST_PAD, V7X_LANES), F32)

    x = x_ref[...]
    h = _rms(x, pre_ref[...]).astype(BF16)

    o_b = 3 * aw
    o_c = o_b + 2 * bw
    zc = _dot(h, win_ref[:, o_c:o_c + 2 * cw])
    _slab_store(fc_ref, C_HIST_PAD, zc[:, 0:cw] * jax.nn.sigmoid(zc[:, cw:2 * cw]))
    base = C_HIST_PAD - (C_KERNEL - 1)
    for r0 in range(0, tt, CONV_ROWS):
        acc = _causal_conv(fc_ref, ccw_ref, base + r0, CONV_ROWS) + ccb_ref[...]
        y_c = jax.nn.silu(_ln(acc, clg_ref[...], clb_ref[...]))
        mix_ref[r0:r0 + CONV_ROWS, aw + bw:aw + bw + cw] = y_c.astype(BF16)
    last_c = _slab_load(fc_ref, base + tt, C_KERNEL - 1)
    nc_ref[...] = last_c
    _slab_store(fc_ref, base, last_c)

    za = _dot(h, win_ref[:, 0:3 * aw])
    _slab_store(fa_ref, A_HIST_PAD, za[:, aw:2 * aw] * za[:, 0:aw])
    base = A_HIST_PAD - (A_KERNEL - 1)
    for r0 in range(0, tt, CONV_ROWS):
        y_a = za[r0:r0 + CONV_ROWS, 2 * aw:3 * aw] * _causal_conv(fa_ref, acw_ref, base + r0, CONV_ROWS)
        mix_ref[r0:r0 + CONV_ROWS, 0:aw] = y_a.astype(BF16)
    last_a = _slab_load(fa_ref, base + tt, A_KERNEL - 1)
    na_ref[...] = last_a
    _slab_store(fa_ref, base, last_a)

    u = _gelu(_dot(h, win_ref[:, o_b:o_b + bw]))
    gv = _gelu(_dot(h, win_ref[:, o_b + bw:o_b + 2 * bw]))
    row_i = lax.broadcasted_iota(jnp.int32, (CHUNK, CHUNK), 0)
    col_j = lax.broadcasted_iota(jnp.int32, (CHUNK, CHUNK), 1)
    n_chunks = tt // CHUNK
    for hh in range(B_HEADS):
        cols = slice(hh * hd, (hh + 1) * hd)
        v = _ln(gv[:, cols], lng_ref[hh:hh + 1, :], lnb_ref[hh:hh + 1, :]).astype(BF16)
        w_tril = jnp.where(col_j <= row_i, ws_ref[hh], 0.0).astype(BF16)
        bias = bias_ref[:, hh:hh + 1]
        v_cat = jnp.concatenate([v[c * CHUNK:(c + 1) * CHUNK, :] for c in range(n_chunks)], axis=1)
        sp = _dot(w_tril, v_cat)
        for c in range(n_chunks):
            rows = slice(c * CHUNK, (c + 1) * CHUNK)
            y_b = u[rows, cols] * (sp[:, c * hd:(c + 1) * hd] + bias)
            mix_ref[rows, aw + hh * hd:aw + (hh + 1) * hd] = y_b.astype(BF16)

    mix = _dot(mix_ref[...], wout_ref[...])
    o_ref[...] = x + _rms(mix, post_ref[...])


def _state_conv(state_ref, w_ref, new):
    k1 = state_ref.shape[0]
    acc = w_ref[k1:k1 + 1, :] * new
    for k in range(k1):
        acc = acc + w_ref[k:k + 1, :] * state_ref[k]
    return acc


def _state_shift(out_ref, state_ref, new):
    k1 = state_ref.shape[0]
    out_ref[0:k1 - 1] = state_ref[1:k1]
    out_ref[k1 - 1] = new


def _mixer_sample_rows(x_ref, sa_ref, sc_ref, pre_ref, post_ref, win_ref, wout_ref, acw_ref,
                       lng_ref, lnb_ref, w00_ref, b0_ref, ccw_ref, ccb_ref, clg_ref, clb_ref,
                       o_ref, na_ref, nc_ref, v_ref, mix_ref):
    n = x_ref.shape[0]
    aw = acw_ref.shape[1]
    bw = w00_ref.shape[1]
    hd = bw // B_HEADS
    cw = ccw_ref.shape[1]
    rows = slice(0, n)
    x = x_ref[...]
    h = _rms(x, pre_ref[...]).astype(BF16)

    za = _dot(h, win_ref[:, 0:3 * aw])
    fa_new = za[:, aw:2 * aw] * za[:, 0:aw]
    conv_a = _state_conv(sa_ref, acw_ref, fa_new)
    mix_ref[rows, 0:aw] = (za[:, 2 * aw:3 * aw] * conv_a).astype(BF16)
    _state_shift(na_ref, sa_ref, fa_new)

    o_b = 3 * aw
    u = _gelu(_dot(h, win_ref[:, o_b:o_b + bw]))
    gv = _gelu(_dot(h, win_ref[:, o_b + bw:o_b + 2 * bw]))
    for hh in range(B_HEADS):
        cols = slice(hh * hd, (hh + 1) * hd)
        v = _ln(gv[:, cols], lng_ref[hh:hh + 1, :], lnb_ref[hh:hh + 1, :])
        v_ref[:, cols] = v
        sp = w00_ref[:, cols] * v + b0_ref[:, cols]
        mix_ref[rows, aw + hh * hd:aw + (hh + 1) * hd] = (u[:, cols] * sp).astype(BF16)

    o_c = o_b + 2 * bw
    zc = _dot(h, win_ref[:, o_c:o_c + 2 * cw])
    fc_new = zc[:, 0:cw] * jax.nn.sigmoid(zc[:, cw:2 * cw])
    acc = _state_conv(sc_ref, ccw_ref, fc_new) + ccb_ref[...]
    y_c = jax.nn.silu(_ln(acc, clg_ref[...], clb_ref[...]))
    mix_ref[rows, aw + bw:aw + bw + cw] = y_c.astype(BF16)
    _state_shift(nc_ref, sc_ref, fc_new)

    mix = _dot(mix_ref[rows, :], wout_ref[...])
    o_ref[...] = x + _rms(mix, post_ref[...])


def _mixer_body(x_ref, xs_ref, sa_ref, sc_ref, pre_ref, post_ref, win32_ref, wout32_ref,
                acw_ref, lng_ref, lnb_ref, ws_ref, bias_ref, w00_ref, b0_ref,
                ccw_ref, ccb_ref, clg_ref, clb_ref,
                o_ref, nap_ref, ncp_ref, os_ref, nas_ref, ncs_ref, v_ref,
                fa_ref, fc_ref, mix_ref, win_ref, wout_ref, *, n_prompt, tiles_per_seq):
    s = pl.program_id(0)

    @pl.when(s == 0)
    def _():
        win_ref[...] = win32_ref[...].astype(BF16)
        wout_ref[...] = wout32_ref[...].astype(BF16)

    @pl.when(s < n_prompt)
    def _():
        _mixer_prompt_rows(lax.rem(s, tiles_per_seq), x_ref, pre_ref, post_ref, win_ref, wout_ref,
                           acw_ref, lng_ref, lnb_ref, ws_ref, bias_ref, ccw_ref, ccb_ref,
                           clg_ref, clb_ref, o_ref, nap_ref, ncp_ref, fa_ref, fc_ref, mix_ref)

    @pl.when(s == n_prompt)
    def _():
        _mixer_sample_rows(xs_ref, sa_ref, sc_ref, pre_ref, post_ref, win_ref, wout_ref, acw_ref,
                           lng_ref, lnb_ref, w00_ref, b0_ref, ccw_ref, ccb_ref, clg_ref, clb_ref,
                           os_ref, nas_ref, ncs_ref, v_ref, mix_ref)


def _mixer(x, xs, sa, sc, pre, post, w_in, w_out, a_cw, lng, lnb, ws, bias_t, w00, b0,
           c_cw, c_cb, clg, clb):
    b, s, d = x.shape
    ns = xs.shape[0]
    d_in = _shape(w_in)[1]
    d_mix = _shape(w_out)[0]
    aw, cw = a_cw.shape[1], c_cw.shape[1]
    bw = w00.shape[1]
    assert ws.shape[1] == CHUNK and ws.shape[2] == CHUNK and CHUNK % CONV_ROWS == 0
    assert aw % V7X_LANES == 0 and cw % V7X_LANES == 0
    args = [x, xs, sa, sc, pre, post, w_in, w_out, a_cw, lng, lnb, ws, bias_t, w00, b0,
            c_cw, c_cb, clg, clb]
    sample_out = [(ns, d), _shape(sa), _shape(sc), (ns, bw)]
    fixed = (sum(_bytes(a) for a in args[1:]) + 2 * sum(_nbytes(o, F32) for o in sample_out)
             + _nbytes((d, d_in), BF16) + _nbytes((d_mix, d), BF16))
    per_row = 4 * _nbytes((d,), F32) + _nbytes((d_mix,), BF16) + _nbytes((aw + cw,), F32)
    tile = _pick_tile(s, fixed, per_row, multiple_of=CHUNK)
    assert ns <= tile
    blocks = fixed + tile * per_row
    tiles_per_seq = s // tile
    n_prompt = b * tiles_per_seq
    last = n_prompt - 1

    def seq_of(i):
        return lax.div(jnp.minimum(i, last), tiles_per_seq)

    def tile_of(i):
        return lax.rem(jnp.minimum(i, last), tiles_per_seq)

    xblk = pl.BlockSpec((None, tile, d), lambda i: (seq_of(i), tile_of(i), 0))
    return pl.pallas_call(
        functools.partial(_mixer_body, n_prompt=n_prompt, tiles_per_seq=tiles_per_seq),
        grid=(n_prompt + 1,),
        in_specs=[xblk] + [_resident(a) for a in args[1:]],
        out_specs=[xblk,
                   pl.BlockSpec((None, A_KERNEL - 1, aw), lambda i: (seq_of(i), 0, 0)),
                   pl.BlockSpec((None, C_KERNEL - 1, cw), lambda i: (seq_of(i), 0, 0))]
                  + [_whole_out(o) for o in sample_out],
        out_shape=[jax.ShapeDtypeStruct((b, s, d), F32),
                   jax.ShapeDtypeStruct((b, A_KERNEL - 1, aw), F32),
                   jax.ShapeDtypeStruct((b, C_KERNEL - 1, cw), F32)]
                  + [jax.ShapeDtypeStruct(o, F32) for o in sample_out],
        scratch_shapes=[pltpu.VMEM((aw // V7X_LANES, A_HIST_PAD + tile, V7X_LANES), F32),
                        pltpu.VMEM((cw // V7X_LANES, C_HIST_PAD + tile, V7X_LANES), F32),
                        pltpu.VMEM((tile, d_mix), BF16),
                        pltpu.VMEM((d, d_in), BF16), pltpu.VMEM((d_mix, d), BF16)],
        compiler_params=pltpu.CompilerParams(
            dimension_semantics=("arbitrary",), vmem_limit_bytes=_vmem_limit(blocks)),
        name="mixer",
    )(*[_arr(a) for a in args])


def kernel(x_prompt, x_sample, state_conv_a, state_conv_c, p_prompt, p_sample, f1_pre, f1_post, f1_wg, f1_wu, f1_wd, m_pre, m_post, w_in, w_out, a_conv_w, b_ln_g, b_ln_b, b_ws, b_bias, c_conv_w, c_conv_b, c_ln_g, c_ln_b, f2_pre, f2_post, f2_wg, f2_wu, f2_wd, e_pre, e_post, e_wg, e_wp):
    depth = f1_wg.shape[0]
    nb, seq, d = x_prompt.shape
    ns = x_sample.shape[0]
    assert x_sample.shape[1] == 1
    hd = b_ws.shape[2]
    bw = B_HEADS * hd

    yp = x_prompt.reshape(nb * seq, d)
    ys = x_sample.reshape(ns, d)
    sa_t = jnp.transpose(state_conv_a, (0, 2, 1, 3))
    sc_t = jnp.transpose(state_conv_c, (0, 2, 1, 3))
    pp = p_prompt.reshape(depth, nb * seq, -1)
    ps = p_sample.reshape(depth, ns, -1)
    a_p, c_p, a_s, c_s, v_s = [], [], [], [], []
    for i in range(depth):
        row = lambda g: g[i].reshape(1, -1)
        w = lambda a: _Layer(a, i)
        f1 = (row(f1_pre), row(f1_post), w(f1_wg), w(f1_wu), w(f1_wd))
        f2 = (row(f2_pre), row(f2_post), w(f2_wg), w(f2_wu), w(f2_wd))
        em = (row(e_pre), row(e_post), w(e_wg), w(e_wp))
        w00 = jnp.repeat(b_ws[i, :, 0, 0], hd).reshape(1, bw)
        b0 = jnp.repeat(b_bias[i, :, 0], hd).reshape(1, bw)

        yp, ys = _ffn(yp, ys, f1)
        yp, na_p, nc_p, ys, na_s, nc_s, vr_s = _mixer(
            yp.reshape(nb, seq, d), ys, w(sa_t), w(sc_t),
            row(m_pre), row(m_post), w(w_in), w(w_out),
            a_conv_w[i], b_ln_g[i], b_ln_b[i], b_ws[i], b_bias[i].T, w00, b0,
            c_conv_w[i], row(c_conv_b), row(c_ln_g), row(c_ln_b))
        yp, ys = _ffn(yp.reshape(nb * seq, d), ys, f2,
                      p=_Layer(pp, i), ps=_Layer(ps, i), embed_w=em)

        a_p.append(na_p)
        c_p.append(nc_p)
        a_s.append(na_s)
        c_s.append(nc_s)
        v_s.append(vr_s.reshape(ns, 1, bw))
    return (yp.reshape(nb, seq, d), ys.reshape(ns, 1, d), jnp.stack(a_p), jnp.stack(c_p),
            jnp.transpose(jnp.stack(a_s), (0, 2, 1, 3)),
            jnp.transpose(jnp.stack(c_s), (0, 2, 1, 3)), jnp.stack(v_s))
```

```python
import functools
import math
from typing import NamedTuple

import jax
import jax.numpy as jnp
from jax import lax
from jax.experimental import pallas as pl
from jax.experimental.pallas import tpu as pltpu

EPS = 1e-6
CHUNK = 128
A_KERNEL = 3
C_KERNEL = 31
B_HEADS = 4

V7X_SUBLANES = 8
V7X_LANES = 128
V7X_MXU_COLS = 256
V7X_VMEM_BYTES = 64 * 1024 * 1024

TILE_CANDIDATES = (1024, 512, 256, 128)
VMEM_HEADROOM = 5 << 20
FFN_SUB_ROWS = 256
MIXER_SUB_ROWS = 512
FF_COLS = V7X_MXU_COLS
CONV_ROWS = 32
A_HIST_PAD = V7X_SUBLANES
C_HIST_PAD = 4 * V7X_SUBLANES

GELU_C = math.sqrt(2.0 / math.pi)
GELU_CUBIC = 0.044715

F32 = jnp.float32
BF16 = jnp.bfloat16


def _vmem_limit(block_bytes):
    return min(block_bytes + 2 * VMEM_HEADROOM, V7X_VMEM_BYTES - (2 << 20))


def _pick_tile(n_rows, fixed_bytes, row_bytes, multiple_of=1):
    for tile in TILE_CANDIDATES:
        fits = fixed_bytes + tile * row_bytes <= V7X_VMEM_BYTES - VMEM_HEADROOM
        if n_rows % tile == 0 and tile % multiple_of == 0 and fits:
            return tile
    raise ValueError("no prompt tile fits VMEM")


def _rms(x, g):
    return x * lax.rsqrt(jnp.mean(x * x, axis=-1, keepdims=True) + EPS) * g


def _ln(x, g, b):
    mu = jnp.mean(x, axis=-1, keepdims=True)
    xc = x - mu
    var = jnp.mean(xc * xc, axis=-1, keepdims=True)
    return xc * lax.rsqrt(var + EPS) * g + b


def _gelu(x):
    inner = x * (GELU_C + (GELU_C * GELU_CUBIC) * (x * x))
    hx = 0.5 * x
    return hx + hx * jnp.tanh(inner)


def _dot(a, b):
    return jnp.dot(a, b, preferred_element_type=F32)


class _Layer(NamedTuple):
    arr: jax.Array
    layer: int


def _shape(a):
    return a.arr.shape[1:] if isinstance(a, _Layer) else a.shape


def _arr(a):
    return a.arr if isinstance(a, _Layer) else a


def _nbytes(shape, dtype):
    n = 1
    for s in shape:
        n *= s
    return n * jnp.dtype(dtype).itemsize


def _bytes(a):
    return _nbytes(_shape(a), _arr(a).dtype)


def _resident(a):
    shape = _shape(a)
    zeros = (0,) * len(shape)
    if isinstance(a, _Layer):
        return pl.BlockSpec((None,) + shape, lambda *_: (a.layer,) + zeros,
                            pipeline_mode=pl.Buffered(1))
    return pl.BlockSpec(shape, lambda *_: zeros, pipeline_mode=pl.Buffered(1))


def _whole_out(shape):
    zeros = (0,) * len(shape)
    return pl.BlockSpec(shape, lambda *_: zeros)


def _rows(a, tile, n_tiles):
    cols = _shape(a)[1]
    last = n_tiles - 1
    if isinstance(a, _Layer):
        return pl.BlockSpec((None, tile, cols), lambda i: (a.layer, jnp.minimum(i, last), 0))
    return pl.BlockSpec((tile, cols), lambda i: (jnp.minimum(i, last), 0))


def _slab_store(ref, row0, val):
    for j in range(ref.shape[0]):
        ref[j, row0:row0 + val.shape[0], :] = val[:, j * V7X_LANES:(j + 1) * V7X_LANES]


def _slab_load(ref, row0, rows):
    return jnp.concatenate([ref[j, row0:row0 + rows, :] for j in range(ref.shape[0])], axis=1)


def _causal_conv(ref, w_ref, row0, rows):
    out = []
    for j in range(ref.shape[0]):
        lanes = slice(j * V7X_LANES, (j + 1) * V7X_LANES)
        acc = w_ref[0:1, lanes] * ref[j, row0:row0 + rows, :]
        for k in range(1, w_ref.shape[0]):
            acc = acc + w_ref[k:k + 1, lanes] * ref[j, row0 + k:row0 + k + rows, :]
        out.append(acc)
    return jnp.concatenate(out, axis=1)


def _ffn_rows(x_ref, p_ref, w, o_ref, act_ref):
    pre_ref, post_ref, wg_ref, wu_ref, wd_ref = w[:5]
    d_ff = wg_ref.shape[1]
    n = x_ref.shape[0]
    sub = min(n, FFN_SUB_ROWS)
    subs = [slice(r0, r0 + sub) for r0 in range(0, n, sub)]
    xs = [x_ref[r, :] for r in subs]
    hs = [_rms(x, pre_ref[...]).astype(BF16) for x in xs]
    for r, h in zip(subs, hs):
        for c0 in range(0, d_ff, FF_COLS):
            g = _dot(h, wg_ref[:, c0:c0 + FF_COLS])
            u = _dot(h, wu_ref[:, c0:c0 + FF_COLS])
            act_ref[r, c0:c0 + FF_COLS] = (jax.nn.silu(g) * u).astype(BF16)
    ys = [_dot(act_ref[r, :], wd_ref[...]) for r in subs]
    xs = [x + 0.5 * _rms(y, post_ref[...]) for x, y in zip(xs, ys)]
    if p_ref is not None:
        epre_ref, epost_ref, ewg_ref, ewp_ref = w[5:]
        hs = [_rms(x, epre_ref[...]).astype(BF16) for x in xs]
        gates = [jax.nn.sigmoid(_dot(h, ewg_ref[...])) for h in hs]
        pes = [_dot(p_ref[r, :].astype(BF16), ewp_ref[...]) for r in subs]
        xs = [x + _rms(gate * pe, epost_ref[...]) for x, gate, pe in zip(xs, gates, pes)]
    for r, x in zip(subs, xs):
        o_ref[r, :] = x


def _ffn_body(*refs, with_embed, n_prompt):
    if with_embed:
        x_ref, xs_ref, p_ref, ps_ref = refs[:4]
        w = refs[4:-3]
    else:
        x_ref, xs_ref = refs[:2]
        p_ref = ps_ref = None
        w = refs[2:-3]
    o_ref, os_ref, act_ref = refs[-3:]
    i = pl.program_id(0)

    @pl.when(i < n_prompt)
    def _():
        _ffn_rows(x_ref, p_ref, w, o_ref, act_ref)

    @pl.when(i == n_prompt)
    def _():
        _ffn_rows(xs_ref, ps_ref, w, os_ref, act_ref)


def _ffn(x, xs, ffn_w, p=None, ps=None, embed_w=None):
    n, d = x.shape
    ns = xs.shape[0]
    d_ff = _shape(ffn_w[2])[1]
    with_embed = embed_w is not None
    weights = list(ffn_w) + (list(embed_w) if with_embed else [])
    fixed = sum(_bytes(a) for a in weights) + 3 * _bytes(xs)
    per_row = 5 * _nbytes((d,), F32) + _nbytes((d_ff,), BF16)
    if with_embed:
        fixed += _bytes(ps)
        per_row += 2 * _nbytes((_shape(p)[1],), F32)
    tile = _pick_tile(n, fixed, per_row)
    assert ns <= tile and d_ff % FF_COLS == 0
    blocks = fixed + tile * per_row
    n_prompt = n // tile
    row = _rows(x, tile, n_prompt)
    args = [x, xs]
    in_specs = [row, _resident(xs)]
    if with_embed:
        args += [p, ps]
        in_specs += [_rows(p, tile, n_prompt), _resident(ps)]
    return pl.pallas_call(
        functools.partial(_ffn_body, with_embed=with_embed, n_prompt=n_prompt),
        grid=(n_prompt + 1,),
        in_specs=in_specs + [_resident(a) for a in weights],
        out_specs=[row, _whole_out((ns, d))],
        out_shape=[jax.ShapeDtypeStruct((n, d), F32), jax.ShapeDtypeStruct((ns, d), F32)],
        scratch_shapes=[pltpu.VMEM((tile, d_ff), BF16)],
        compiler_params=pltpu.CompilerParams(
            dimension_semantics=("arbitrary",), vmem_limit_bytes=_vmem_limit(blocks)),
        name="ffn_embed" if with_embed else "ffn",
    )(*[_arr(a) for a in args + weights])


def _mixer_prompt_rows(t, x_ref, pre_ref, post_ref, win_ref, wout_ref, acw_ref, lng_ref, lnb_ref,
                       ws_ref, bias_ref, ccw_ref, ccb_ref, clg_ref, clb_ref,
                       o_ref, na_ref, nc_ref, fa_ref, fc_ref, mix_ref):
    tt = x_ref.shape[0]
    aw = acw_ref.shape[1]
    bw = ws_ref.shape[0] * ws_ref.shape[1]
    hd = ws_ref.shape[1]
    cw = ccw_ref.shape[1]

    @pl.when(t == 0)
    def _():
        fa_ref[:, 0:A_HIST_PAD, :] = jnp.zeros((fa_ref.shape[0], A_HIST_PAD, V7X_LANES), F32)
        fc_ref[:, 0:C_HIST_PAD, :] = jnp.zeros((fc_ref.shape[0], C_HIST_PAD, V7X_LANES), F32)

    sub = min(tt, MIXER_SUB_ROWS)
    subs = [slice(r0, r0 + sub) for r0 in range(0, tt, sub)]
    hs = [_rms(x_ref[r, :], pre_ref[...]).astype(BF16) for r in subs]
    o_b = 3 * aw
    o_c = o_b + 2 * bw

    for r, h in zip(subs, hs):
        zc = _dot(h, win_ref[:, o_c:o_c + 2 * cw])
        _slab_store(fc_ref, C_HIST_PAD + r.start, zc[:, 0:cw] * jax.nn.sigmoid(zc[:, cw:2 * cw]))
    base = C_HIST_PAD - (C_KERNEL - 1)
    for r0 in range(0, tt, CONV_ROWS):
        acc = _causal_conv(fc_ref, ccw_ref, base + r0, CONV_ROWS) + ccb_ref[...]
        y_c = jax.nn.silu(_ln(acc, clg_ref[...], clb_ref[...]))
        mix_ref[r0:r0 + CONV_ROWS, aw + bw:aw + bw + cw] = y_c.astype(BF16)
    last_c = _slab_load(fc_ref, base + tt, C_KERNEL - 1)
    nc_ref[...] = last_c
    _slab_store(fc_ref, base, last_c)

    base = A_HIST_PAD - (A_KERNEL - 1)
    for r, h in zip(subs, hs):
        za = _dot(h, win_ref[:, 0:3 * aw])
        _slab_store(fa_ref, A_HIST_PAD + r.start, za[:, aw:2 * aw] * za[:, 0:aw])
        for q0 in range(0, sub, CONV_ROWS):
            r0 = r.start + q0
            y_a = za[q0:q0 + CONV_ROWS, 2 * aw:3 * aw] * _causal_conv(fa_ref, acw_ref, base + r0, CONV_ROWS)
            mix_ref[r0:r0 + CONV_ROWS, 0:aw] = y_a.astype(BF16)
    last_a = _slab_load(fa_ref, base + tt, A_KERNEL - 1)
    na_ref[...] = last_a
    _slab_store(fa_ref, base, last_a)

    row_i = lax.broadcasted_iota(jnp.int32, (CHUNK, CHUNK), 0)
    col_j = lax.broadcasted_iota(jnp.int32, (CHUNK, CHUNK), 1)
    w_trils = [jnp.where(col_j <= row_i, ws_ref[hh], 0.0).astype(BF16) for hh in range(B_HEADS)]
    n_chunks = sub // CHUNK
    us = [_gelu(_dot(h, win_ref[:, o_b:o_b + bw])) for h in hs]
    gvs = [_gelu(_dot(h, win_ref[:, o_b + bw:o_b + 2 * bw])) for h in hs]
    for r, u, gv in zip(subs, us, gvs):
        for hh in range(B_HEADS):
            cols = slice(hh * hd, (hh + 1) * hd)
            v = _ln(gv[:, cols], lng_ref[hh:hh + 1, :], lnb_ref[hh:hh + 1, :]).astype(BF16)
            bias = bias_ref[:, hh:hh + 1]
            v_cat = jnp.concatenate(
                [v[c * CHUNK:(c + 1) * CHUNK, :] for c in range(n_chunks)], axis=1)
            sp = _dot(w_trils[hh], v_cat)
            for c in range(n_chunks):
                y_b = u[c * CHUNK:(c + 1) * CHUNK, cols] * (sp[:, c * hd:(c + 1) * hd] + bias)
                rows = slice(r.start + c * CHUNK, r.start + (c + 1) * CHUNK)
                mix_ref[rows, aw + hh * hd:aw + (hh + 1) * hd] = y_b.astype(BF16)

    mixes = [_dot(mix_ref[r, :], wout_ref[...]) for r in subs]
    for r, mix in zip(subs, mixes):
        o_ref[r, :] = x_ref[r, :] + _rms(mix, post_ref[...])


def _state_conv(state_ref, w_ref, new):
    k1 = state_ref.shape[0]
    acc = w_ref[k1:k1 + 1, :] * new
    for k in range(k1):
        acc = acc + w_ref[k:k + 1, :] * state_ref[k]
    return acc


def _state_shift(out_ref, state_ref, new):
    k1 = state_ref.shape[0]
    out_ref[0:k1 - 1] = state_ref[1:k1]
    out_ref[k1 - 1] = new


def _mixer_sample_rows(x_ref, sa_ref, sc_ref, pre_ref, post_ref, win_ref, wout_ref, acw_ref,
                       lng_ref, lnb_ref, w00_ref, b0_ref, ccw_ref, ccb_ref, clg_ref, clb_ref,
                       o_ref, na_ref, nc_ref, v_ref, mix_ref):
    n = x_ref.shape[0]
    aw = acw_ref.shape[1]
    bw = w00_ref.shape[1]
    hd = bw // B_HEADS
    cw = ccw_ref.shape[1]
    rows = slice(0, n)
    x = x_ref[...]
    h = _rms(x, pre_ref[...]).astype(BF16)

    za = _dot(h, win_ref[:, 0:3 * aw])
    fa_new = za[:, aw:2 * aw] * za[:, 0:aw]
    conv_a = _state_conv(sa_ref, acw_ref, fa_new)
    mix_ref[rows, 0:aw] = (za[:, 2 * aw:3 * aw] * conv_a).astype(BF16)
    _state_shift(na_ref, sa_ref, fa_new)

    o_b = 3 * aw
    u = _gelu(_dot(h, win_ref[:, o_b:o_b + bw]))
    gv = _gelu(_dot(h, win_ref[:, o_b + bw:o_b + 2 * bw]))
    for hh in range(B_HEADS):
        cols = slice(hh * hd, (hh + 1) * hd)
        v = _ln(gv[:, cols], lng_ref[hh:hh + 1, :], lnb_ref[hh:hh + 1, :])
        v_ref[:, cols] = v
        sp = w00_ref[:, cols] * v + b0_ref[:, cols]
        mix_ref[rows, aw + hh * hd:aw + (hh + 1) * hd] = (u[:, cols] * sp).astype(BF16)

    o_c = o_b + 2 * bw
    zc = _dot(h, win_ref[:, o_c:o_c + 2 * cw])
    fc_new = zc[:, 0:cw] * jax.nn.sigmoid(zc[:, cw:2 * cw])
    acc = _state_conv(sc_ref, ccw_ref, fc_new) + ccb_ref[...]
    y_c = jax.nn.silu(_ln(acc, clg_ref[...], clb_ref[...]))
    mix_ref[rows, aw + bw:aw + bw + cw] = y_c.astype(BF16)
    _state_shift(nc_ref, sc_ref, fc_new)

    mix = _dot(mix_ref[rows, :], wout_ref[...])
    o_ref[...] = x + _rms(mix, post_ref[...])


def _mixer_body(x_ref, xs_ref, sa_ref, sc_ref, pre_ref, post_ref, win32_ref, wout32_ref,
                acw_ref, lng_ref, lnb_ref, ws_ref, bias_ref, w00_ref, b0_ref,
                ccw_ref, ccb_ref, clg_ref, clb_ref,
                o_ref, nap_ref, ncp_ref, os_ref, nas_ref, ncs_ref, v_ref,
                fa_ref, fc_ref, mix_ref, win_ref, wout_ref, *, n_prompt, tiles_per_seq):
    s = pl.program_id(0)

    @pl.when(s == 0)
    def _():
        win_ref[...] = win32_ref[...].astype(BF16)
        wout_ref[...] = wout32_ref[...].astype(BF16)

    @pl.when(s < n_prompt)
    def _():
        _mixer_prompt_rows(lax.rem(s, tiles_per_seq), x_ref, pre_ref, post_ref, win_ref, wout_ref,
                           acw_ref, lng_ref, lnb_ref, ws_ref, bias_ref, ccw_ref, ccb_ref,
                           clg_ref, clb_ref, o_ref, nap_ref, ncp_ref, fa_ref, fc_ref, mix_ref)

    @pl.when(s == n_prompt)
    def _():
        _mixer_sample_rows(xs_ref, sa_ref, sc_ref, pre_ref, post_ref, win_ref, wout_ref, acw_ref,
                           lng_ref, lnb_ref, w00_ref, b0_ref, ccw_ref, ccb_ref, clg_ref, clb_ref,
                           os_ref, nas_ref, ncs_ref, v_ref, mix_ref)


def _mixer(x, xs, sa, sc, pre, post, w_in, w_out, a_cw, lng, lnb, ws, bias_t, w00, b0,
           c_cw, c_cb, clg, clb):
    b, s, d = x.shape
    ns = xs.shape[0]
    d_in = _shape(w_in)[1]
    d_mix = _shape(w_out)[0]
    aw, cw = a_cw.shape[1], c_cw.shape[1]
    bw = w00.shape[1]
    assert ws.shape[1] == CHUNK and ws.shape[2] == CHUNK and CHUNK % CONV_ROWS == 0
    assert aw % V7X_LANES == 0 and cw % V7X_LANES == 0
    args = [x, xs, sa, sc, pre, post, w_in, w_out, a_cw, lng, lnb, ws, bias_t, w00, b0,
            c_cw, c_cb, clg, clb]
    sample_out = [(ns, d), _shape(sa), _shape(sc), (ns, bw)]
    fixed = (sum(_bytes(a) for a in args[1:]) + 2 * sum(_nbytes(o, F32) for o in sample_out)
             + _nbytes((d, d_in), BF16) + _nbytes((d_mix, d), BF16))
    per_row = 4 * _nbytes((d,), F32) + _nbytes((d_mix,), BF16) + _nbytes((aw + cw,), F32)
    tile = _pick_tile(s, fixed, per_row, multiple_of=CHUNK)
    assert ns <= tile
    blocks = fixed + tile * per_row
    tiles_per_seq = s // tile
    n_prompt = b * tiles_per_seq
    last = n_prompt - 1

    def seq_of(i):
        return lax.div(jnp.minimum(i, last), tiles_per_seq)

    def tile_of(i):
        return lax.rem(jnp.minimum(i, last), tiles_per_seq)

    xblk = pl.BlockSpec((None, tile, d), lambda i: (seq_of(i), tile_of(i), 0))
    return pl.pallas_call(
        functools.partial(_mixer_body, n_prompt=n_prompt, tiles_per_seq=tiles_per_seq),
        grid=(n_prompt + 1,),
        in_specs=[xblk] + [_resident(a) for a in args[1:]],
        out_specs=[xblk,
                   pl.BlockSpec((None, A_KERNEL - 1, aw), lambda i: (seq_of(i), 0, 0)),
                   pl.BlockSpec((None, C_KERNEL - 1, cw), lambda i: (seq_of(i), 0, 0))]
                  + [_whole_out(o) for o in sample_out],
        out_shape=[jax.ShapeDtypeStruct((b, s, d), F32),
                   jax.ShapeDtypeStruct((b, A_KERNEL - 1, aw), F32),
                   jax.ShapeDtypeStruct((b, C_KERNEL - 1, cw), F32)]
                  + [jax.ShapeDtypeStruct(o, F32) for o in sample_out],
        scratch_shapes=[pltpu.VMEM((aw // V7X_LANES, A_HIST_PAD + tile, V7X_LANES), F32),
                        pltpu.VMEM((cw // V7X_LANES, C_HIST_PAD + tile, V7X_LANES), F32),
                        pltpu.VMEM((tile, d_mix), BF16),
                        pltpu.VMEM((d, d_in), BF16), pltpu.VMEM((d_mix, d), BF16)],
        compiler_params=pltpu.CompilerParams(
            dimension_semantics=("arbitrary",), vmem_limit_bytes=_vmem_limit(blocks)),
        name="mixer",
    )(*[_arr(a) for a in args])


def kernel(x_prompt, x_sample, state_conv_a, state_conv_c, p_prompt, p_sample, f1_pre, f1_post, f1_wg, f1_wu, f1_wd, m_pre, m_post, w_in, w_out, a_conv_w, b_ln_g, b_ln_b, b_ws, b_bias, c_conv_w, c_conv_b, c_ln_g, c_ln_b, f2_pre, f2_post, f2_wg, f2_wu, f2_wd, e_pre, e_post, e_wg, e_wp):
    depth = f1_wg.shape[0]
    nb, seq, d = x_prompt.shape
    ns = x_sample.shape[0]
    assert x_sample.shape[1] == 1
    hd = b_ws.shape[2]
    bw = B_HEADS * hd

    yp = x_prompt.reshape(nb * seq, d)
    ys = x_sample.reshape(ns, d)
    sa_t = jnp.transpose(state_conv_a, (0, 2, 1, 3))
    sc_t = jnp.transpose(state_conv_c, (0, 2, 1, 3))
    pp = p_prompt.reshape(depth, nb * seq, -1)
    ps = p_sample.reshape(depth, ns, -1)
    a_p, c_p, a_s, c_s, v_s = [], [], [], [], []
    for i in range(depth):
        row = lambda g: g[i].reshape(1, -1)
        w = lambda a: _Layer(a, i)
        f1 = (row(f1_pre), row(f1_post), w(f1_wg), w(f1_wu), w(f1_wd))
        f2 = (row(f2_pre), row(f2_post), w(f2_wg), w(f2_wu), w(f2_wd))
        em = (row(e_pre), row(e_post), w(e_wg), w(e_wp))
        w00 = jnp.repeat(b_ws[i, :, 0, 0], hd).reshape(1, bw)
        b0 = jnp.repeat(b_bias[i, :, 0], hd).reshape(1, bw)

        yp, ys = _ffn(yp, ys, f1)
        yp, na_p, nc_p, ys, na_s, nc_s, vr_s = _mixer(
            yp.reshape(nb, seq, d), ys, w(sa_t), w(sc_t),
            row(m_pre), row(m_post), w(w_in), w(w_out),
            a_conv_w[i], b_ln_g[i], b_ln_b[i], b_ws[i], b_bias[i].T, w00, b0,
            c_conv_w[i], row(c_conv_b), row(c_ln_g), row(c_ln_b))
        yp, ys = _ffn(yp.reshape(nb * seq, d), ys, f2,
                      p=_Layer(pp, i), ps=_Layer(ps, i), embed_w=em)

        a_p.append(na_p)
        c_p.append(nc_p)
        a_s.append(na_s)
        c_s.append(nc_s)
        v_s.append(vr_s.reshape(ns, 1, bw))
    return (yp.reshape(nb, seq, d), ys.reshape(ns, 1, d), jnp.stack(a_p), jnp.stack(c_p),
            jnp.transpose(jnp.stack(a_s), (0, 2, 1, 3)),
            jnp.transpose(jnp.stack(c_s), (0, 2, 1, 3)), jnp.stack(v_s))
```

```python
import functools
import math
from typing import NamedTuple

import jax
import jax.numpy as jnp
from jax import lax
from jax.experimental import pallas as pl
from jax.experimental.pallas import tpu as pltpu

EPS = 1e-6
CHUNK = 128
A_KERNEL = 3
C_KERNEL = 31
B_HEADS = 4

V7X_SUBLANES = 8
V7X_LANES = 128
V7X_MXU_COLS = 256
V7X_VMEM_BYTES = 64 * 1024 * 1024

TILE_CANDIDATES = (1024, 512, 256, 128)
VMEM_HEADROOM = 5 << 20
FFN_SUB_ROWS = 256
FFN_ROW_PARAMS = (0, 1, 5, 6)
MIXER_SUB_ROWS = 512
FF_COLS = V7X_MXU_COLS
CONV_ROWS = 32
A_HIST_PAD = V7X_SUBLANES
C_HIST_PAD = 4 * V7X_SUBLANES

GELU_C = math.sqrt(2.0 / math.pi)
GELU_CUBIC = 0.044715

F32 = jnp.float32
BF16 = jnp.bfloat16


def _vmem_limit(block_bytes):
    return min(block_bytes + 2 * VMEM_HEADROOM, V7X_VMEM_BYTES - (2 << 20))


def _pick_tile(n_rows, fixed_bytes, row_bytes, multiple_of=1):
    for tile in TILE_CANDIDATES:
        fits = fixed_bytes + tile * row_bytes <= V7X_VMEM_BYTES - VMEM_HEADROOM
        if n_rows % tile == 0 and tile % multiple_of == 0 and fits:
            return tile
    raise ValueError("no prompt tile fits VMEM")


def _rms(x, g):
    return x * lax.rsqrt(jnp.mean(x * x, axis=-1, keepdims=True) + EPS) * g


def _ln(x, g, b):
    mu = jnp.mean(x, axis=-1, keepdims=True)
    xc = x - mu
    var = jnp.mean(xc * xc, axis=-1, keepdims=True)
    return xc * lax.rsqrt(var + EPS) * g + b


def _gelu(x):
    inner = x * (GELU_C + (GELU_C * GELU_CUBIC) * (x * x))
    hx = 0.5 * x
    return hx + hx * jnp.tanh(inner)


def _dot(a, b):
    return jnp.dot(a, b, preferred_element_type=F32)


class _Layer(NamedTuple):
    arr: jax.Array
    layer: int


class _Row(NamedTuple):
    arr: jax.Array
    layer: int


class _RowOf:
    def __init__(self, ref, layer):
        self.ref, self.layer = ref, layer

    def __getitem__(self, idx):
        assert idx is Ellipsis
        return self.ref[self.layer:self.layer + 1, :]


def _shape(a):
    if isinstance(a, _Layer):
        return a.arr.shape[1:]
    return a.arr.shape if isinstance(a, _Row) else a.shape


def _arr(a):
    return a.arr if isinstance(a, (_Layer, _Row)) else a


def _nbytes(shape, dtype):
    n = 1
    for s in shape:
        n *= s
    return n * jnp.dtype(dtype).itemsize


def _bytes(a):
    return _nbytes(_shape(a), _arr(a).dtype)


def _resident(a):
    shape = _shape(a)
    zeros = (0,) * len(shape)
    if isinstance(a, _Layer):
        return pl.BlockSpec((None,) + shape, lambda *_: (a.layer,) + zeros,
                            pipeline_mode=pl.Buffered(1))
    return pl.BlockSpec(shape, lambda *_: zeros, pipeline_mode=pl.Buffered(1))


def _whole_out(shape):
    zeros = (0,) * len(shape)
    return pl.BlockSpec(shape, lambda *_: zeros)


def _rows(a, tile, n_tiles):
    cols = _shape(a)[1]
    last = n_tiles - 1
    if isinstance(a, _Layer):
        return pl.BlockSpec((None, tile, cols), lambda i: (a.layer, jnp.minimum(i, last), 0))
    return pl.BlockSpec((tile, cols), lambda i: (jnp.minimum(i, last), 0))


def _slab_store(ref, row0, val):
    for j in range(ref.shape[0]):
        ref[j, row0:row0 + val.shape[0], :] = val[:, j * V7X_LANES:(j + 1) * V7X_LANES]


def _slab_load(ref, row0, rows):
    return jnp.concatenate([ref[j, row0:row0 + rows, :] for j in range(ref.shape[0])], axis=1)


def _causal_conv(ref, w_ref, row0, rows):
    out = []
    for j in range(ref.shape[0]):
        lanes = slice(j * V7X_LANES, (j + 1) * V7X_LANES)
        acc = w_ref[0:1, lanes] * ref[j, row0:row0 + rows, :]
        for k in range(1, w_ref.shape[0]):
            acc = acc + w_ref[k:k + 1, lanes] * ref[j, row0 + k:row0 + k + rows, :]
        out.append(acc)
    return jnp.concatenate(out, axis=1)


def _ffn_rows(x_ref, p_ref, w, o_ref, act_ref):
    pre_ref, post_ref, wg_ref, wu_ref, wd_ref = w[:5]
    d_ff = wg_ref.shape[1]
    n = x_ref.shape[0]
    sub = min(n, FFN_SUB_ROWS)
    subs = [slice(r0, r0 + sub) for r0 in range(0, n, sub)]
    xs = [x_ref[r, :] for r in subs]
    hs = [_rms(x, pre_ref[...]).astype(BF16) for x in xs]
    for r, h in zip(subs, hs):
        for c0 in range(0, d_ff, FF_COLS):
            g = _dot(h, wg_ref[:, c0:c0 + FF_COLS])
            u = _dot(h, wu_ref[:, c0:c0 + FF_COLS])
            act_ref[r, c0:c0 + FF_COLS] = (jax.nn.silu(g) * u).astype(BF16)
    ys = [_dot(act_ref[r, :], wd_ref[...]) for r in subs]
    xs = [x + 0.5 * _rms(y, post_ref[...]) for x, y in zip(xs, ys)]
    if p_ref is not None:
        epre_ref, epost_ref, ewg_ref, ewp_ref = w[5:]
        hs = [_rms(x, epre_ref[...]).astype(BF16) for x in xs]
        gates = [jax.nn.sigmoid(_dot(h, ewg_ref[...])) for h in hs]
        pes = [_dot(p_ref[r, :].astype(BF16), ewp_ref[...]) for r in subs]
        xs = [x + _rms(gate * pe, epost_ref[...]) for x, gate, pe in zip(xs, gates, pes)]
    for r, x in zip(subs, xs):
        o_ref[r, :] = x


def _ffn_body(*refs, with_embed, n_prompt, layer):
    if with_embed:
        x_ref, xs_ref, p_ref, ps_ref = refs[:4]
        w = list(refs[4:-3])
    else:
        x_ref, xs_ref = refs[:2]
        p_ref = ps_ref = None
        w = list(refs[2:-3])
    for k in FFN_ROW_PARAMS:
        if k < len(w):
            w[k] = _RowOf(w[k], layer)
    o_ref, os_ref, act_ref = refs[-3:]
    i = pl.program_id(0)

    @pl.when(i < n_prompt)
    def _():
        _ffn_rows(x_ref, p_ref, w, o_ref, act_ref)

    @pl.when(i == n_prompt)
    def _():
        _ffn_rows(xs_ref, ps_ref, w, os_ref, act_ref)


def _ffn(x, xs, ffn_w, layer, p=None, ps=None, embed_w=None):
    n, d = x.shape
    ns = xs.shape[0]
    d_ff = _shape(ffn_w[2])[1]
    with_embed = embed_w is not None
    weights = list(ffn_w) + (list(embed_w) if with_embed else [])
    fixed = sum(_bytes(a) for a in weights) + 3 * _bytes(xs)
    per_row = 5 * _nbytes((d,), F32) + _nbytes((d_ff,), BF16)
    if with_embed:
        fixed += _bytes(ps)
        per_row += 2 * _nbytes((_shape(p)[1],), F32)
    tile = _pick_tile(n, fixed, per_row)
    assert ns <= tile and d_ff % FF_COLS == 0
    blocks = fixed + tile * per_row
    n_prompt = n // tile
    row = _rows(x, tile, n_prompt)
    args = [x, xs]
    in_specs = [row, _resident(xs)]
    if with_embed:
        args += [p, ps]
        in_specs += [_rows(p, tile, n_prompt), _resident(ps)]
    return pl.pallas_call(
        functools.partial(_ffn_body, with_embed=with_embed, n_prompt=n_prompt, layer=layer),
        grid=(n_prompt + 1,),
        in_specs=in_specs + [_resident(a) for a in weights],
        out_specs=[row, _whole_out((ns, d))],
        out_shape=[jax.ShapeDtypeStruct((n, d), F32), jax.ShapeDtypeStruct((ns, d), F32)],
        scratch_shapes=[pltpu.VMEM((tile, d_ff), BF16)],
        compiler_params=pltpu.CompilerParams(
            dimension_semantics=("arbitrary",), vmem_limit_bytes=_vmem_limit(blocks)),
        name="ffn_embed" if with_embed else "ffn",
    )(*[_arr(a) for a in args + weights])


def _mixer_prompt_rows(t, x_ref, pre_ref, post_ref, win_ref, wout_ref, acw_ref, lng_ref, lnb_ref,
                       ws_ref, bias_ref, ccw_ref, ccb_ref, clg_ref, clb_ref,
                       o_ref, na_ref, nc_ref, fa_ref, fc_ref, mix_ref):
    tt = x_ref.shape[0]
    aw = acw_ref.shape[1]
    bw = ws_ref.shape[0] * ws_ref.shape[1]
    hd = ws_ref.shape[1]
    cw = ccw_ref.shape[1]

    @pl.when(t == 0)
    def _():
        fa_ref[:, 0:A_HIST_PAD, :] = jnp.zeros((fa_ref.shape[0], A_HIST_PAD, V7X_LANES), F32)
        fc_ref[:, 0:C_HIST_PAD, :] = jnp.zeros((fc_ref.shape[0], C_HIST_PAD, V7X_LANES), F32)

    sub = min(tt, MIXER_SUB_ROWS)
    subs = [slice(r0, r0 + sub) for r0 in range(0, tt, sub)]
    hs = [_rms(x_ref[r, :], pre_ref[...]).astype(BF16) for r in subs]
    o_b = 3 * aw
    o_c = o_b + 2 * bw

    for r, h in zip(subs, hs):
        zc = _dot(h, win_ref[:, o_c:o_c + 2 * cw])
        _slab_store(fc_ref, C_HIST_PAD + r.start, zc[:, 0:cw] * jax.nn.sigmoid(zc[:, cw:2 * cw]))
    base = C_HIST_PAD - (C_KERNEL - 1)
    for r0 in range(0, tt, CONV_ROWS):
        acc = _causal_conv(fc_ref, ccw_ref, base + r0, CONV_ROWS) + ccb_ref[...]
        y_c = jax.nn.silu(_ln(acc, clg_ref[...], clb_ref[...]))
        mix_ref[r0:r0 + CONV_ROWS, aw + bw:aw + bw + cw] = y_c.astype(BF16)
    last_c = _slab_load(fc_ref, base + tt, C_KERNEL - 1)
    nc_ref[...] = last_c
    _slab_store(fc_ref, base, last_c)

    base = A_HIST_PAD - (A_KERNEL - 1)
    for r, h in zip(subs, hs):
        za = _dot(h, win_ref[:, 0:3 * aw])
        _slab_store(fa_ref, A_HIST_PAD + r.start, za[:, aw:2 * aw] * za[:, 0:aw])
        for q0 in range(0, sub, CONV_ROWS):
            r0 = r.start + q0
            y_a = za[q0:q0 + CONV_ROWS, 2 * aw:3 * aw] * _causal_conv(fa_ref, acw_ref, base + r0, CONV_ROWS)
            mix_ref[r0:r0 + CONV_ROWS, 0:aw] = y_a.astype(BF16)
    last_a = _slab_load(fa_ref, base + tt, A_KERNEL - 1)
    na_ref[...] = last_a
    _slab_store(fa_ref, base, last_a)

    row_i = lax.broadcasted_iota(jnp.int32, (CHUNK, CHUNK), 0)
    col_j = lax.broadcasted_iota(jnp.int32, (CHUNK, CHUNK), 1)
    w_trils = [jnp.where(col_j <= row_i, ws_ref[hh], 0.0).astype(BF16) for hh in range(B_HEADS)]
    n_chunks = sub // CHUNK
    us = [_gelu(_dot(h, win_ref[:, o_b:o_b + bw])) for h in hs]
    gvs = [_gelu(_dot(h, win_ref[:, o_b + bw:o_b + 2 * bw])) for h in hs]
    for r, u, gv in zip(subs, us, gvs):
        for hh in range(B_HEADS):
            cols = slice(hh * hd, (hh + 1) * hd)
            v = _ln(gv[:, cols], lng_ref[hh:hh + 1, :], lnb_ref[hh:hh + 1, :]).astype(BF16)
            bias = bias_ref[:, hh:hh + 1]
            v_cat = jnp.concatenate(
                [v[c * CHUNK:(c + 1) * CHUNK, :] for c in range(n_chunks)], axis=1)
            sp = _dot(w_trils[hh], v_cat)
            for c in range(n_chunks):
                y_b = u[c * CHUNK:(c + 1) * CHUNK, cols] * (sp[:, c * hd:(c + 1) * hd] + bias)
                rows = slice(r.start + c * CHUNK, r.start + (c + 1) * CHUNK)
                mix_ref[rows, aw + hh * hd:aw + (hh + 1) * hd] = y_b.astype(BF16)

    mixes = [_dot(mix_ref[r, :], wout_ref[...]) for r in subs]
    for r, mix in zip(subs, mixes):
        o_ref[r, :] = x_ref[r, :] + _rms(mix, post_ref[...])


def _state_conv(state_ref, w_ref, new):
    k1 = state_ref.shape[0]
    acc = w_ref[k1:k1 + 1, :] * new
    for k in range(k1):
        acc = acc + w_ref[k:k + 1, :] * state_ref[k]
    return acc


def _state_shift(out_ref, state_ref, new):
    k1 = state_ref.shape[0]
    out_ref[0:k1 - 1] = state_ref[1:k1]
    out_ref[k1 - 1] = new


def _mixer_sample_rows(x_ref, sa_ref, sc_ref, pre_ref, post_ref, win_ref, wout_ref, acw_ref,
                       lng_ref, lnb_ref, w00_ref, b0_ref, ccw_ref, ccb_ref, clg_ref, clb_ref,
                       o_ref, na_ref, nc_ref, v_ref, mix_ref):
    n = x_ref.shape[0]
    aw = acw_ref.shape[1]
    bw = w00_ref.shape[1]
    hd = bw // B_HEADS
    cw = ccw_ref.shape[1]
    rows = slice(0, n)
    x = x_ref[...]
    h = _rms(x, pre_ref[...]).astype(BF16)

    za = _dot(h, win_ref[:, 0:3 * aw])
    fa_new = za[:, aw:2 * aw] * za[:, 0:aw]
    conv_a = _state_conv(sa_ref, acw_ref, fa_new)
    mix_ref[rows, 0:aw] = (za[:, 2 * aw:3 * aw] * conv_a).astype(BF16)
    _state_shift(na_ref, sa_ref, fa_new)

    o_b = 3 * aw
    u = _gelu(_dot(h, win_ref[:, o_b:o_b + bw]))
    gv = _gelu(_dot(h, win_ref[:, o_b + bw:o_b + 2 * bw]))
    for hh in range(B_HEADS):
        cols = slice(hh * hd, (hh + 1) * hd)
        v = _ln(gv[:, cols], lng_ref[hh:hh + 1, :], lnb_ref[hh:hh + 1, :])
        v_ref[:, cols] = v
        sp = w00_ref[:, cols] * v + b0_ref[:, cols]
        mix_ref[rows, aw + hh * hd:aw + (hh + 1) * hd] = (u[:, cols] * sp).astype(BF16)

    o_c = o_b + 2 * bw
    zc = _dot(h, win_ref[:, o_c:o_c + 2 * cw])
    fc_new = zc[:, 0:cw] * jax.nn.sigmoid(zc[:, cw:2 * cw])
    acc = _state_conv(sc_ref, ccw_ref, fc_new) + ccb_ref[...]
    y_c = jax.nn.silu(_ln(acc, clg_ref[...], clb_ref[...]))
    mix_ref[rows, aw + bw:aw + bw + cw] = y_c.astype(BF16)
    _state_shift(nc_ref, sc_ref, fc_new)

    mix = _dot(mix_ref[rows, :], wout_ref[...])
    o_ref[...] = x + _rms(mix, post_ref[...])


def _mixer_body(x_ref, xs_ref, sa_ref, sc_ref, pre_ref, post_ref, win32_ref, wout32_ref,
                acw_ref, lng_ref, lnb_ref, ws_ref, bias_ref, w00_ref, b0_ref,
                ccw_ref, ccb_ref, clg_ref, clb_ref,
                o_ref, nap_ref, ncp_ref, os_ref, nas_ref, ncs_ref, v_ref,
                fa_ref, fc_ref, mix_ref, win_ref, wout_ref, *, n_prompt, tiles_per_seq, layer):
    s = pl.program_id(0)
    pre_ref, post_ref, ccb_ref, clg_ref, clb_ref = (
        _RowOf(r, layer) for r in (pre_ref, post_ref, ccb_ref, clg_ref, clb_ref))

    @pl.when(s == 0)
    def _():
        win_ref[...] = win32_ref[...].astype(BF16)
        wout_ref[...] = wout32_ref[...].astype(BF16)

    @pl.when(s < n_prompt)
    def _():
        _mixer_prompt_rows(lax.rem(s, tiles_per_seq), x_ref, pre_ref, post_ref, win_ref, wout_ref,
                           acw_ref, lng_ref, lnb_ref, ws_ref, bias_ref, ccw_ref, ccb_ref,
                           clg_ref, clb_ref, o_ref, nap_ref, ncp_ref, fa_ref, fc_ref, mix_ref)

    @pl.when(s == n_prompt)
    def _():
        _mixer_sample_rows(xs_ref, sa_ref, sc_ref, pre_ref, post_ref, win_ref, wout_ref, acw_ref,
                           lng_ref, lnb_ref, w00_ref, b0_ref, ccw_ref, ccb_ref, clg_ref, clb_ref,
                           os_ref, nas_ref, ncs_ref, v_ref, mix_ref)


def _mixer(x, xs, sa, sc, pre, post, w_in, w_out, a_cw, lng, lnb, ws, bias_t, w00, b0,
           c_cw, c_cb, clg, clb, layer):
    b, s, d = x.shape
    ns = xs.shape[0]
    d_in = _shape(w_in)[1]
    d_mix = _shape(w_out)[0]
    aw, cw = _shape(a_cw)[1], _shape(c_cw)[1]
    bw = _shape(w00)[1]
    assert _shape(ws)[1] == CHUNK and _shape(ws)[2] == CHUNK and CHUNK % CONV_ROWS == 0
    assert aw % V7X_LANES == 0 and cw % V7X_LANES == 0
    args = [x, xs, sa, sc, pre, post, w_in, w_out, a_cw, lng, lnb, ws, bias_t, w00, b0,
            c_cw, c_cb, clg, clb]
    sample_out = [(ns, d), _shape(sa), _shape(sc), (ns, bw)]
    fixed = (sum(_bytes(a) for a in args[1:]) + 2 * sum(_nbytes(o, F32) for o in sample_out)
             + _nbytes((d, d_in), BF16) + _nbytes((d_mix, d), BF16))
    per_row = 4 * _nbytes((d,), F32) + _nbytes((d_mix,), BF16) + _nbytes((aw + cw,), F32)
    tile = _pick_tile(s, fixed, per_row, multiple_of=CHUNK)
    assert ns <= tile
    blocks = fixed + tile * per_row
    tiles_per_seq = s // tile
    n_prompt = b * tiles_per_seq
    last = n_prompt - 1

    def seq_of(i):
        return lax.div(jnp.minimum(i, last), tiles_per_seq)

    def tile_of(i):
        return lax.rem(jnp.minimum(i, last), tiles_per_seq)

    xblk = pl.BlockSpec((None, tile, d), lambda i: (seq_of(i), tile_of(i), 0))
    return pl.pallas_call(
        functools.partial(_mixer_body, n_prompt=n_prompt, tiles_per_seq=tiles_per_seq,
                          layer=layer),
        grid=(n_prompt + 1,),
        in_specs=[xblk] + [_resident(a) for a in args[1:]],
        out_specs=[xblk,
                   pl.BlockSpec((None, A_KERNEL - 1, aw), lambda i: (seq_of(i), 0, 0)),
                   pl.BlockSpec((None, C_KERNEL - 1, cw), lambda i: (seq_of(i), 0, 0))]
                  + [_whole_out(o) for o in sample_out],
        out_shape=[jax.ShapeDtypeStruct((b, s, d), F32),
                   jax.ShapeDtypeStruct((b, A_KERNEL - 1, aw), F32),
                   jax.ShapeDtypeStruct((b, C_KERNEL - 1, cw), F32)]
                  + [jax.ShapeDtypeStruct(o, F32) for o in sample_out],
        scratch_shapes=[pltpu.VMEM((aw // V7X_LANES, A_HIST_PAD + tile, V7X_LANES), F32),
                        pltpu.VMEM((cw // V7X_LANES, C_HIST_PAD + tile, V7X_LANES), F32),
                        pltpu.VMEM((tile, d_mix), BF16),
                        pltpu.VMEM((d, d_in), BF16), pltpu.VMEM((d_mix, d), BF16)],
        compiler_params=pltpu.CompilerParams(
            dimension_semantics=("arbitrary",), vmem_limit_bytes=_vmem_limit(blocks)),
        name="mixer",
    )(*[_arr(a) for a in args])


def kernel(x_prompt, x_sample, state_conv_a, state_conv_c, p_prompt, p_sample, f1_pre, f1_post, f1_wg, f1_wu, f1_wd, m_pre, m_post, w_in, w_out, a_conv_w, b_ln_g, b_ln_b, b_ws, b_bias, c_conv_w, c_conv_b, c_ln_g, c_ln_b, f2_pre, f2_post, f2_wg, f2_wu, f2_wd, e_pre, e_post, e_wg, e_wp):
    depth = f1_wg.shape[0]
    nb, seq, d = x_prompt.shape
    ns = x_sample.shape[0]
    assert x_sample.shape[1] == 1
    hd = b_ws.shape[2]
    bw = B_HEADS * hd

    yp = x_prompt.reshape(nb * seq, d)
    ys = x_sample.reshape(ns, d)
    sa_t = jnp.transpose(state_conv_a, (0, 2, 1, 3))
    sc_t = jnp.transpose(state_conv_c, (0, 2, 1, 3))
    bias_t = jnp.transpose(b_bias, (0, 2, 1))
    w00 = jnp.repeat(b_ws[:, :, 0, 0], hd, axis=1).reshape(depth, 1, bw)
    b0 = jnp.repeat(b_bias[:, :, 0], hd, axis=1).reshape(depth, 1, bw)
    pp = p_prompt.reshape(depth, nb * seq, -1)
    ps = p_sample.reshape(depth, ns, -1)
    a_p, c_p, a_s, c_s, v_s = [], [], [], [], []
    for i in range(depth):
        row = lambda g: _Row(g, i)
        w = lambda a: _Layer(a, i)
        f1 = (row(f1_pre), row(f1_post), w(f1_wg), w(f1_wu), w(f1_wd))
        f2 = (row(f2_pre), row(f2_post), w(f2_wg), w(f2_wu), w(f2_wd))
        em = (row(e_pre), row(e_post), w(e_wg), w(e_wp))

        yp, ys = _ffn(yp, ys, f1, i)
        yp, na_p, nc_p, ys, na_s, nc_s, vr_s = _mixer(
            yp.reshape(nb, seq, d), ys, w(sa_t), w(sc_t),
            row(m_pre), row(m_post), w(w_in), w(w_out),
            w(a_conv_w), w(b_ln_g), w(b_ln_b), w(b_ws), w(bias_t), w(w00), w(b0),
            w(c_conv_w), row(c_conv_b), row(c_ln_g), row(c_ln_b), layer=i)
        yp, ys = _ffn(yp.reshape(nb * seq, d), ys, f2, i,
                      p=_Layer(pp, i), ps=_Layer(ps, i), embed_w=em)

        a_p.append(na_p)
        c_p.append(nc_p)
        a_s.append(na_s)
        c_s.append(nc_s)
        v_s.append(vr_s.reshape(ns, 1, bw))
    return (yp.reshape(nb, seq, d), ys.reshape(ns, 1, d), jnp.stack(a_p), jnp.stack(c_p),
            jnp.transpose(jnp.stack(a_s), (0, 2, 1, 3)),
            jnp.transpose(jnp.stack(c_s), (0, 2, 1, 3)), jnp.stack(v_s))
```

```python
import functools
import math
from typing import NamedTuple

import jax
import jax.numpy as jnp
from jax import lax
from jax.experimental import pallas as pl
from jax.experimental.pallas import tpu as pltpu

EPS = 1e-6
CHUNK = 128
A_KERNEL = 3
C_KERNEL = 31
B_HEADS = 4

V7X_SUBLANES = 8
V7X_LANES = 128
V7X_MXU_COLS = 256
V7X_VMEM_BYTES = 64 * 1024 * 1024

TILE_CANDIDATES = (1024, 512, 256, 128)
VMEM_HEADROOM = 5 << 20
VMEM_UNREQUESTABLE = 2 << 20
FFN_SUB_ROWS = 256
FFN_ROW_PARAMS = (0, 1, 5, 6)
MIXER_SUB_ROWS = 512
FF_COLS = V7X_MXU_COLS
CONV_ROWS = 32
A_HIST_PAD = V7X_SUBLANES
C_HIST_PAD = 4 * V7X_SUBLANES

GELU_C = math.sqrt(2.0 / math.pi)
GELU_CUBIC = 0.044715

F32 = jnp.float32
BF16 = jnp.bfloat16


def _vmem_limit(block_bytes):
    return min(block_bytes + 2 * VMEM_HEADROOM, V7X_VMEM_BYTES - VMEM_UNREQUESTABLE)


def _pick_tile(n_rows, fixed_bytes, row_bytes, multiple_of=1):
    for tile in TILE_CANDIDATES:
        fits = fixed_bytes + tile * row_bytes <= V7X_VMEM_BYTES - VMEM_HEADROOM
        if n_rows % tile == 0 and tile % multiple_of == 0 and fits:
            return tile
    raise ValueError("no prompt tile fits VMEM")


def _rms(x, g):
    return x * lax.rsqrt(jnp.mean(x * x, axis=-1, keepdims=True) + EPS) * g


def _ln(x, g, b):
    mu = jnp.mean(x, axis=-1, keepdims=True)
    xc = x - mu
    var = jnp.mean(xc * xc, axis=-1, keepdims=True)
    return xc * lax.rsqrt(var + EPS) * g + b


def _gelu(x):
    two_z = x * (2.0 * GELU_C + (2.0 * GELU_C * GELU_CUBIC) * (x * x))
    return x * jax.nn.sigmoid(two_z)


def _dot(a, b):
    return jnp.dot(a, b, preferred_element_type=F32)


class _Layer(NamedTuple):
    arr: jax.Array
    layer: int


class _Row(NamedTuple):
    arr: jax.Array
    layer: int


class _RowOf:
    def __init__(self, ref, layer):
        self.ref, self.layer = ref, layer

    def __getitem__(self, idx):
        assert idx is Ellipsis
        return self.ref[self.layer:self.layer + 1, :]


def _shape(a):
    if isinstance(a, _Layer):
        return a.arr.shape[1:]
    return a.arr.shape if isinstance(a, _Row) else a.shape


def _arr(a):
    return a.arr if isinstance(a, (_Layer, _Row)) else a


def _nbytes(shape, dtype):
    n = 1
    for s in shape:
        n *= s
    return n * jnp.dtype(dtype).itemsize


def _bytes(a):
    return _nbytes(_shape(a), _arr(a).dtype)


def _resident(a):
    shape = _shape(a)
    zeros = (0,) * len(shape)
    if isinstance(a, _Layer):
        return pl.BlockSpec((None,) + shape, lambda *_: (a.layer,) + zeros,
                            pipeline_mode=pl.Buffered(1))
    return pl.BlockSpec(shape, lambda *_: zeros, pipeline_mode=pl.Buffered(1))


def _whole_out(shape):
    zeros = (0,) * len(shape)
    return pl.BlockSpec(shape, lambda *_: zeros)


def _rows(a, tile, n_tiles):
    cols = _shape(a)[1]
    last = n_tiles - 1
    if isinstance(a, _Layer):
        return pl.BlockSpec((None, tile, cols), lambda i: (a.layer, jnp.minimum(i, last), 0))
    return pl.BlockSpec((tile, cols), lambda i: (jnp.minimum(i, last), 0))


def _slab_store(ref, row0, val):
    for j in range(ref.shape[0]):
        ref[j, row0:row0 + val.shape[0], :] = val[:, j * V7X_LANES:(j + 1) * V7X_LANES]


def _slab_load(ref, row0, rows):
    return jnp.concatenate([ref[j, row0:row0 + rows, :] for j in range(ref.shape[0])], axis=1)


def _causal_conv(ref, w_ref, row0, rows):
    out = []
    for j in range(ref.shape[0]):
        lanes = slice(j * V7X_LANES, (j + 1) * V7X_LANES)
        acc = w_ref[0:1, lanes] * ref[j, row0:row0 + rows, :]
        for k in range(1, w_ref.shape[0]):
            acc = acc + w_ref[k:k + 1, lanes] * ref[j, row0 + k:row0 + k + rows, :]
        out.append(acc)
    return jnp.concatenate(out, axis=1)


def _ffn_rows(x_ref, p_ref, w, o_ref, act_ref):
    pre_ref, post_ref, wg_ref, wu_ref, wd_ref = w[:5]
    d_ff = wg_ref.shape[1]
    n = x_ref.shape[0]
    sub = min(n, FFN_SUB_ROWS)
    subs = [slice(r0, r0 + sub) for r0 in range(0, n, sub)]
    xs = [x_ref[r, :] for r in subs]
    hs = [_rms(x, pre_ref[...]).astype(BF16) for x in xs]
    for r, h in zip(subs, hs):
        for c0 in range(0, d_ff, FF_COLS):
            g = _dot(h, wg_ref[:, c0:c0 + FF_COLS])
            u = _dot(h, wu_ref[:, c0:c0 + FF_COLS])
            act_ref[r, c0:c0 + FF_COLS] = (jax.nn.silu(g) * u).astype(BF16)
    ys = [_dot(act_ref[r, :], wd_ref[...]) for r in subs]
    xs = [x + 0.5 * _rms(y, post_ref[...]) for x, y in zip(xs, ys)]
    if p_ref is not None:
        epre_ref, epost_ref, ewg_ref, ewp_ref = w[5:]
        hs = [_rms(x, epre_ref[...]).astype(BF16) for x in xs]
        gates = [jax.nn.sigmoid(_dot(h, ewg_ref[...])) for h in hs]
        pes = [_dot(p_ref[r, :].astype(BF16), ewp_ref[...]) for r in subs]
        xs = [x + _rms(gate * pe, epost_ref[...]) for x, gate, pe in zip(xs, gates, pes)]
    for r, x in zip(subs, xs):
        o_ref[r, :] = x


def _ffn_body(*refs, with_embed, n_prompt, layer):
    if with_embed:
        x_ref, xs_ref, p_ref, ps_ref = refs[:4]
        w = list(refs[4:-3])
    else:
        x_ref, xs_ref = refs[:2]
        p_ref = ps_ref = None
        w = list(refs[2:-3])
    for k in FFN_ROW_PARAMS:
        if k < len(w):
            w[k] = _RowOf(w[k], layer)
    o_ref, os_ref, act_ref = refs[-3:]
    i = pl.program_id(0)

    @pl.when(i < n_prompt)
    def _():
        _ffn_rows(x_ref, p_ref, w, o_ref, act_ref)

    @pl.when(i == n_prompt)
    def _():
        _ffn_rows(xs_ref, ps_ref, w, os_ref, act_ref)


def _ffn(x, xs, ffn_w, layer, p=None, ps=None, embed_w=None):
    n, d = x.shape
    ns = xs.shape[0]
    d_ff = _shape(ffn_w[2])[1]
    with_embed = embed_w is not None
    weights = list(ffn_w) + (list(embed_w) if with_embed else [])
    fixed = sum(_bytes(a) for a in weights) + 3 * _bytes(xs)
    per_row = 5 * _nbytes((d,), F32) + _nbytes((d_ff,), BF16)
    if with_embed:
        fixed += _bytes(ps)
        per_row += 2 * _nbytes((_shape(p)[1],), F32)
    tile = _pick_tile(n, fixed, per_row)
    assert ns <= tile and d_ff % FF_COLS == 0
    blocks = fixed + tile * per_row
    n_prompt = n // tile
    row = _rows(x, tile, n_prompt)
    args = [x, xs]
    in_specs = [row, _resident(xs)]
    if with_embed:
        args += [p, ps]
        in_specs += [_rows(p, tile, n_prompt), _resident(ps)]
    return pl.pallas_call(
        functools.partial(_ffn_body, with_embed=with_embed, n_prompt=n_prompt, layer=layer),
        grid=(n_prompt + 1,),
        in_specs=in_specs + [_resident(a) for a in weights],
        out_specs=[row, _whole_out((ns, d))],
        out_shape=[jax.ShapeDtypeStruct((n, d), F32), jax.ShapeDtypeStruct((ns, d), F32)],
        scratch_shapes=[pltpu.VMEM((tile, d_ff), BF16)],
        compiler_params=pltpu.CompilerParams(
            dimension_semantics=("arbitrary",), vmem_limit_bytes=_vmem_limit(blocks)),
        name="ffn_embed" if with_embed else "ffn",
    )(*[_arr(a) for a in args + weights])


def _mixer_prompt_rows(t, x_ref, pre_ref, post_ref, win_ref, wout_ref, acw_ref, lng_ref, lnb_ref,
                       ws_ref, bias_ref, ccw_ref, ccb_ref, clg_ref, clb_ref,
                       o_ref, na_ref, nc_ref, fa_ref, fc_ref, mix_ref):
    tt = x_ref.shape[0]
    aw = acw_ref.shape[1]
    bw = ws_ref.shape[0] * ws_ref.shape[1]
    hd = ws_ref.shape[1]
    cw = ccw_ref.shape[1]

    @pl.when(t == 0)
    def _():
        fa_ref[:, 0:A_HIST_PAD, :] = jnp.zeros((fa_ref.shape[0], A_HIST_PAD, V7X_LANES), F32)
        fc_ref[:, 0:C_HIST_PAD, :] = jnp.zeros((fc_ref.shape[0], C_HIST_PAD, V7X_LANES), F32)

    sub = min(tt, MIXER_SUB_ROWS)
    subs = [slice(r0, r0 + sub) for r0 in range(0, tt, sub)]
    hs = [_rms(x_ref[r, :], pre_ref[...]).astype(BF16) for r in subs]
    o_b = 3 * aw
    o_c = o_b + 2 * bw

    for r, h in zip(subs, hs):
        zc = _dot(h, win_ref[:, o_c:o_c + 2 * cw])
        _slab_store(fc_ref, C_HIST_PAD + r.start, zc[:, 0:cw] * jax.nn.sigmoid(zc[:, cw:2 * cw]))
    base = C_HIST_PAD - (C_KERNEL - 1)
    for r0 in range(0, tt, CONV_ROWS):
        acc = _causal_conv(fc_ref, ccw_ref, base + r0, CONV_ROWS) + ccb_ref[...]
        y_c = jax.nn.silu(_ln(acc, clg_ref[...], clb_ref[...]))
        mix_ref[r0:r0 + CONV_ROWS, aw + bw:aw + bw + cw] = y_c.astype(BF16)
    last_c = _slab_load(fc_ref, base + tt, C_KERNEL - 1)
    nc_ref[...] = last_c
    _slab_store(fc_ref, base, last_c)

    base = A_HIST_PAD - (A_KERNEL - 1)
    for r, h in zip(subs, hs):
        za = _dot(h, win_ref[:, 0:3 * aw])
        _slab_store(fa_ref, A_HIST_PAD + r.start, za[:, aw:2 * aw] * za[:, 0:aw])
        for q0 in range(0, sub, CONV_ROWS):
            r0 = r.start + q0
            y_a = za[q0:q0 + CONV_ROWS, 2 * aw:3 * aw] * _causal_conv(fa_ref, acw_ref, base + r0, CONV_ROWS)
            mix_ref[r0:r0 + CONV_ROWS, 0:aw] = y_a.astype(BF16)
    last_a = _slab_load(fa_ref, base + tt, A_KERNEL - 1)
    na_ref[...] = last_a
    _slab_store(fa_ref, base, last_a)

    row_i = lax.broadcasted_iota(jnp.int32, (CHUNK, CHUNK), 0)
    col_j = lax.broadcasted_iota(jnp.int32, (CHUNK, CHUNK), 1)
    w_trils = [jnp.where(col_j <= row_i, ws_ref[hh], 0.0).astype(BF16) for hh in range(B_HEADS)]
    n_chunks = sub // CHUNK
    us = [_gelu(_dot(h, win_ref[:, o_b:o_b + bw])) for h in hs]
    gvs = [_gelu(_dot(h, win_ref[:, o_b + bw:o_b + 2 * bw])) for h in hs]
    for r, u, gv in zip(subs, us, gvs):
        for hh in range(B_HEADS):
            cols = slice(hh * hd, (hh + 1) * hd)
            v = _ln(gv[:, cols], lng_ref[hh:hh + 1, :], lnb_ref[hh:hh + 1, :]).astype(BF16)
            bias = bias_ref[:, hh:hh + 1]
            v_cat = jnp.concatenate(
                [v[c * CHUNK:(c + 1) * CHUNK, :] for c in range(n_chunks)], axis=1)
            sp = _dot(w_trils[hh], v_cat)
            for c in range(n_chunks):
                y_b = u[c * CHUNK:(c + 1) * CHUNK, cols] * (sp[:, c * hd:(c + 1) * hd] + bias)
                rows = slice(r.start + c * CHUNK, r.start + (c + 1) * CHUNK)
                mix_ref[rows, aw + hh * hd:aw + (hh + 1) * hd] = y_b.astype(BF16)

    mixes = [_dot(mix_ref[r, :], wout_ref[...]) for r in subs]
    for r, mix in zip(subs, mixes):
        o_ref[r, :] = x_ref[r, :] + _rms(mix, post_ref[...])


def _state_conv(state_ref, w_ref, new):
    k1 = state_ref.shape[0]
    acc = w_ref[k1:k1 + 1, :] * new
    for k in range(k1):
        acc = acc + w_ref[k:k + 1, :] * state_ref[k]
    return acc


def _state_shift(out_ref, state_ref, new):
    k1 = state_ref.shape[0]
    out_ref[0:k1 - 1] = state_ref[1:k1]
    out_ref[k1 - 1] = new


def _mixer_sample_rows(x_ref, sa_ref, sc_ref, pre_ref, post_ref, win_ref, wout_ref, acw_ref,
                       lng_ref, lnb_ref, w00_ref, b0_ref, ccw_ref, ccb_ref, clg_ref, clb_ref,
                       o_ref, na_ref, nc_ref, v_ref, mix_ref):
    n = x_ref.shape[0]
    aw = acw_ref.shape[1]
    bw = w00_ref.shape[1]
    hd = bw // B_HEADS
    cw = ccw_ref.shape[1]
    rows = slice(0, n)
    x = x_ref[...]
    h = _rms(x, pre_ref[...]).astype(BF16)

    za = _dot(h, win_ref[:, 0:3 * aw])
    fa_new = za[:, aw:2 * aw] * za[:, 0:aw]
    conv_a = _state_conv(sa_ref, acw_ref, fa_new)
    mix_ref[rows, 0:aw] = (za[:, 2 * aw:3 * aw] * conv_a).astype(BF16)
    _state_shift(na_ref, sa_ref, fa_new)

    o_b = 3 * aw
    u = _gelu(_dot(h, win_ref[:, o_b:o_b + bw]))
    gv = _gelu(_dot(h, win_ref[:, o_b + bw:o_b + 2 * bw]))
    for hh in range(B_HEADS):
        cols = slice(hh * hd, (hh + 1) * hd)
        v = _ln(gv[:, cols], lng_ref[hh:hh + 1, :], lnb_ref[hh:hh + 1, :])
        v_ref[:, cols] = v
        sp = w00_ref[:, cols] * v + b0_ref[:, cols]
        mix_ref[rows, aw + hh * hd:aw + (hh + 1) * hd] = (u[:, cols] * sp).astype(BF16)

    o_c = o_b + 2 * bw
    zc = _dot(h, win_ref[:, o_c:o_c + 2 * cw])
    fc_new = zc[:, 0:cw] * jax.nn.sigmoid(zc[:, cw:2 * cw])
    acc = _state_conv(sc_ref, ccw_ref, fc_new) + ccb_ref[...]
    y_c = jax.nn.silu(_ln(acc, clg_ref[...], clb_ref[...]))
    mix_ref[rows, aw + bw:aw + bw + cw] = y_c.astype(BF16)
    _state_shift(nc_ref, sc_ref, fc_new)

    mix = _dot(mix_ref[rows, :], wout_ref[...])
    o_ref[...] = x + _rms(mix, post_ref[...])


def _mixer_body(x_ref, xs_ref, sa_ref, sc_ref, pre_ref, post_ref, win32_ref, wout32_ref,
                acw_ref, lng_ref, lnb_ref, ws_ref, bias_ref, w00_ref, b0_ref,
                ccw_ref, ccb_ref, clg_ref, clb_ref,
                o_ref, nap_ref, ncp_ref, os_ref, nas_ref, ncs_ref, v_ref,
                fa_ref, fc_ref, mix_ref, win_ref, wout_ref, *, n_prompt, tiles_per_seq, layer):
    s = pl.program_id(0)
    pre_ref, post_ref, ccb_ref, clg_ref, clb_ref = (
        _RowOf(r, layer) for r in (pre_ref, post_ref, ccb_ref, clg_ref, clb_ref))

    @pl.when(s == 0)
    def _():
        win_ref[...] = win32_ref[...].astype(BF16)
        wout_ref[...] = wout32_ref[...].astype(BF16)

    @pl.when(s < n_prompt)
    def _():
        _mixer_prompt_rows(lax.rem(s, tiles_per_seq), x_ref, pre_ref, post_ref, win_ref, wout_ref,
                           acw_ref, lng_ref, lnb_ref, ws_ref, bias_ref, ccw_ref, ccb_ref,
                           clg_ref, clb_ref, o_ref, nap_ref, ncp_ref, fa_ref, fc_ref, mix_ref)

    @pl.when(s == n_prompt)
    def _():
        _mixer_sample_rows(xs_ref, sa_ref, sc_ref, pre_ref, post_ref, win_ref, wout_ref, acw_ref,
                           lng_ref, lnb_ref, w00_ref, b0_ref, ccw_ref, ccb_ref, clg_ref, clb_ref,
                           os_ref, nas_ref, ncs_ref, v_ref, mix_ref)


def _mixer(x, xs, sa, sc, pre, post, w_in, w_out, a_cw, lng, lnb, ws, bias_t, w00, b0,
           c_cw, c_cb, clg, clb, layer):
    b, s, d = x.shape
    ns = xs.shape[0]
    d_in = _shape(w_in)[1]
    d_mix = _shape(w_out)[0]
    aw, cw = _shape(a_cw)[1], _shape(c_cw)[1]
    bw = _shape(w00)[1]
    assert _shape(ws)[1] == CHUNK and _shape(ws)[2] == CHUNK and CHUNK % CONV_ROWS == 0
    assert aw % V7X_LANES == 0 and cw % V7X_LANES == 0
    args = [x, xs, sa, sc, pre, post, w_in, w_out, a_cw, lng, lnb, ws, bias_t, w00, b0,
            c_cw, c_cb, clg, clb]
    sample_out = [(ns, d), _shape(sa), _shape(sc), (ns, bw)]
    fixed = (sum(_bytes(a) for a in args[1:]) + 2 * sum(_nbytes(o, F32) for o in sample_out)
             + _nbytes((d, d_in), BF16) + _nbytes((d_mix, d), BF16))
    per_row = 4 * _nbytes((d,), F32) + _nbytes((d_mix,), BF16) + _nbytes((aw + cw,), F32)
    tile = _pick_tile(s, fixed, per_row, multiple_of=CHUNK)
    assert ns <= tile
    blocks = fixed + tile * per_row
    tiles_per_seq = s // tile
    n_prompt = b * tiles_per_seq
    last = n_prompt - 1

    def seq_of(i):
        return lax.div(jnp.minimum(i, last), tiles_per_seq)

    def tile_of(i):
        return lax.rem(jnp.minimum(i, last), tiles_per_seq)

    xblk = pl.BlockSpec((None, tile, d), lambda i: (seq_of(i), tile_of(i), 0))
    return pl.pallas_call(
        functools.partial(_mixer_body, n_prompt=n_prompt, tiles_per_seq=tiles_per_seq,
                          layer=layer),
        grid=(n_prompt + 1,),
        in_specs=[xblk] + [_resident(a) for a in args[1:]],
        out_specs=[xblk,
                   pl.BlockSpec((None, A_KERNEL - 1, aw), lambda i: (seq_of(i), 0, 0)),
                   pl.BlockSpec((None, C_KERNEL - 1, cw), lambda i: (seq_of(i), 0, 0))]
                  + [_whole_out(o) for o in sample_out],
        out_shape=[jax.ShapeDtypeStruct((b, s, d), F32),
                   jax.ShapeDtypeStruct((b, A_KERNEL - 1, aw), F32),
                   jax.ShapeDtypeStruct((b, C_KERNEL - 1, cw), F32)]
                  + [jax.ShapeDtypeStruct(o, F32) for o in sample_out],
        scratch_shapes=[pltpu.VMEM((aw // V7X_LANES, A_HIST_PAD + tile, V7X_LANES), F32),
                        pltpu.VMEM((cw // V7X_LANES, C_HIST_PAD + tile, V7X_LANES), F32),
                        pltpu.VMEM((tile, d_mix), BF16),
                        pltpu.VMEM((d, d_in), BF16), pltpu.VMEM((d_mix, d), BF16)],
        compiler_params=pltpu.CompilerParams(
            dimension_semantics=("arbitrary",), vmem_limit_bytes=_vmem_limit(blocks)),
        name="mixer",
    )(*[_arr(a) for a in args])


def kernel(x_prompt, x_sample, state_conv_a, state_conv_c, p_prompt, p_sample, f1_pre, f1_post, f1_wg, f1_wu, f1_wd, m_pre, m_post, w_in, w_out, a_conv_w, b_ln_g, b_ln_b, b_ws, b_bias, c_conv_w, c_conv_b, c_ln_g, c_ln_b, f2_pre, f2_post, f2_wg, f2_wu, f2_wd, e_pre, e_post, e_wg, e_wp):
    depth = f1_wg.shape[0]
    nb, seq, d = x_prompt.shape
    ns = x_sample.shape[0]
    assert x_sample.shape[1] == 1
    hd = b_ws.shape[2]
    bw = B_HEADS * hd

    yp = x_prompt.reshape(nb * seq, d)
    ys = x_sample.reshape(ns, d)
    sa_t = jnp.transpose(state_conv_a, (0, 2, 1, 3))
    sc_t = jnp.transpose(state_conv_c, (0, 2, 1, 3))
    bias_t = jnp.transpose(b_bias, (0, 2, 1))
    w00 = jnp.repeat(b_ws[:, :, 0, 0], hd, axis=1).reshape(depth, 1, bw)
    b0 = jnp.repeat(b_bias[:, :, 0], hd, axis=1).reshape(depth, 1, bw)
    pp = p_prompt.reshape(depth, nb * seq, -1)
    ps = p_sample.reshape(depth, ns, -1)
    a_p, c_p, a_s, c_s, v_s = [], [], [], [], []
    for i in range(depth):
        row = lambda g: _Row(g, i)
        w = lambda a: _Layer(a, i)
        f1 = (row(f1_pre), row(f1_post), w(f1_wg), w(f1_wu), w(f1_wd))
        f2 = (row(f2_pre), row(f2_post), w(f2_wg), w(f2_wu), w(f2_wd))
        em = (row(e_pre), row(e_post), w(e_wg), w(e_wp))

        yp, ys = _ffn(yp, ys, f1, i)
        yp, na_p, nc_p, ys, na_s, nc_s, vr_s = _mixer(
            yp.reshape(nb, seq, d), ys, w(sa_t), w(sc_t),
            row(m_pre), row(m_post), w(w_in), w(w_out),
            w(a_conv_w), w(b_ln_g), w(b_ln_b), w(b_ws), w(bias_t), w(w00), w(b0),
            w(c_conv_w), row(c_conv_b), row(c_ln_g), row(c_ln_b), layer=i)
        yp, ys = _ffn(yp.reshape(nb * seq, d), ys, f2, i,
                      p=_Layer(pp, i), ps=_Layer(ps, i), embed_w=em)

        a_p.append(na_p)
        c_p.append(nc_p)
        a_s.append(na_s)
        c_s.append(nc_s)
        v_s.append(vr_s.reshape(ns, 1, bw))
    return (yp.reshape(nb, seq, d), ys.reshape(ns, 1, d), jnp.stack(a_p), jnp.stack(c_p),
            jnp.transpose(jnp.stack(a_s), (0, 2, 1, 3)),
            jnp.transpose(jnp.stack(c_s), (0, 2, 1, 3)), jnp.stack(v_s))
```

```python
import functools
import math
from typing import NamedTuple

import jax
import jax.numpy as jnp
from jax import lax
from jax.experimental import pallas as pl
from jax.experimental.pallas import tpu as pltpu

EPS = 1e-6
CHUNK = 128
A_KERNEL = 3
C_KERNEL = 31
B_HEADS = 4

V7X_SUBLANES = 8
V7X_LANES = 128
V7X_MXU_COLS = 256
V7X_VMEM_BYTES = 64 * 1024 * 1024

TILE_CANDIDATES = (1024, 512, 256, 128)
VMEM_HEADROOM = 5 << 20
FFN_SUB_ROWS = 256
FFN_ROW_PARAMS = (0, 1, 5, 6)
MIXER_SUB_ROWS = 512
FF_COLS = V7X_MXU_COLS
CONV_ROWS = 32
A_HIST_PAD = V7X_SUBLANES
C_HIST_PAD = 4 * V7X_SUBLANES

GELU_C = math.sqrt(2.0 / math.pi)
GELU_CUBIC = 0.044715

F32 = jnp.float32
BF16 = jnp.bfloat16


def _vmem_limit(block_bytes):
    return min(block_bytes + 2 * VMEM_HEADROOM, V7X_VMEM_BYTES - (2 << 20))


def _pick_tile(n_rows, fixed_bytes, row_bytes, multiple_of=1):
    for tile in TILE_CANDIDATES:
        fits = fixed_bytes + tile * row_bytes <= V7X_VMEM_BYTES - VMEM_HEADROOM
        if n_rows % tile == 0 and tile % multiple_of == 0 and fits:
            return tile
    raise ValueError("no prompt tile fits VMEM")


def _rms(x, g):
    return x * lax.rsqrt(jnp.mean(x * x, axis=-1, keepdims=True) + EPS) * g


def _ln(x, g, b):
    mu = jnp.mean(x, axis=-1, keepdims=True)
    xc = x - mu
    var = jnp.mean(xc * xc, axis=-1, keepdims=True)
    return xc * lax.rsqrt(var + EPS) * g + b


def _gelu(x):
    inner = x * (GELU_C + (GELU_C * GELU_CUBIC) * (x * x))
    hx = 0.5 * x
    return hx + hx * jnp.tanh(inner)


def _dot(a, b):
    return jnp.dot(a, b, preferred_element_type=F32)


class _Layer(NamedTuple):
    arr: jax.Array
    layer: int


class _Row(NamedTuple):
    arr: jax.Array
    layer: int


class _RowOf:
    def __init__(self, ref, layer):
        self.ref, self.layer = ref, layer

    def __getitem__(self, idx):
        assert idx is Ellipsis
        return self.ref[self.layer:self.layer + 1, :]


def _shape(a):
    if isinstance(a, _Layer):
        return a.arr.shape[1:]
    return a.arr.shape if isinstance(a, _Row) else a.shape


def _arr(a):
    return a.arr if isinstance(a, (_Layer, _Row)) else a


def _nbytes(shape, dtype):
    n = 1
    for s in shape:
        n *= s
    return n * jnp.dtype(dtype).itemsize


def _bytes(a):
    return _nbytes(_shape(a), _arr(a).dtype)


def _resident(a):
    shape = _shape(a)
    zeros = (0,) * len(shape)
    if isinstance(a, _Layer):
        return pl.BlockSpec((None,) + shape, lambda *_: (a.layer,) + zeros,
                            pipeline_mode=pl.Buffered(1))
    return pl.BlockSpec(shape, lambda *_: zeros, pipeline_mode=pl.Buffered(1))


def _whole_out(shape):
    zeros = (0,) * len(shape)
    return pl.BlockSpec(shape, lambda *_: zeros)


def _rows(a, tile, n_tiles):
    cols = _shape(a)[1]
    last = n_tiles - 1
    if isinstance(a, _Layer):
        return pl.BlockSpec((None, tile, cols), lambda i: (a.layer, jnp.minimum(i, last), 0))
    return pl.BlockSpec((tile, cols), lambda i: (jnp.minimum(i, last), 0))


def _slab_store(ref, row0, val):
    for j in range(ref.shape[0]):
        ref[j, row0:row0 + val.shape[0], :] = val[:, j * V7X_LANES:(j + 1) * V7X_LANES]


def _slab_load(ref, row0, rows):
    return jnp.concatenate([ref[j, row0:row0 + rows, :] for j in range(ref.shape[0])], axis=1)


def _causal_conv(ref, w_ref, row0, rows):
    out = []
    for j in range(ref.shape[0]):
        lanes = slice(j * V7X_LANES, (j + 1) * V7X_LANES)
        acc = w_ref[0:1, lanes] * ref[j, row0:row0 + rows, :]
        for k in range(1, w_ref.shape[0]):
            acc = acc + w_ref[k:k + 1, lanes] * ref[j, row0 + k:row0 + k + rows, :]
        out.append(acc)
    return jnp.concatenate(out, axis=1)


def _ffn_rows(x_ref, p_ref, w, o_ref, act_ref):
    pre_ref, post_ref, wg_ref, wu_ref, wd_ref = w[:5]
    d_ff = wg_ref.shape[1]
    n = x_ref.shape[0]
    sub = min(n, FFN_SUB_ROWS)
    subs = [slice(r0, r0 + sub) for r0 in range(0, n, sub)]
    if p_ref is not None:
        epre_ref, epost_ref, ewg_ref, ewp_ref = w[5:]
        pes = [_dot(p_ref[r, :].astype(BF16), ewp_ref[...]) for r in subs]
    xs = [x_ref[r, :] for r in subs]
    hs = [_rms(x, pre_ref[...]).astype(BF16) for x in xs]
    for r, h in zip(subs, hs):
        for c0 in range(0, d_ff, FF_COLS):
            g = _dot(h, wg_ref[:, c0:c0 + FF_COLS])
            u = _dot(h, wu_ref[:, c0:c0 + FF_COLS])
            act_ref[r, c0:c0 + FF_COLS] = (jax.nn.silu(g) * u).astype(BF16)
    ys = [_dot(act_ref[r, :], wd_ref[...]) for r in subs]
    xs = [x + 0.5 * _rms(y, post_ref[...]) for x, y in zip(xs, ys)]
    if p_ref is not None:
        hs = [_rms(x, epre_ref[...]).astype(BF16) for x in xs]
        gates = [jax.nn.sigmoid(_dot(h, ewg_ref[...])) for h in hs]
        xs = [x + _rms(gate * pe, epost_ref[...]) for x, gate, pe in zip(xs, gates, pes)]
    for r, x in zip(subs, xs):
        o_ref[r, :] = x


def _ffn_body(*refs, with_embed, n_prompt, layer):
    if with_embed:
        x_ref, xs_ref, p_ref, ps_ref = refs[:4]
        w = list(refs[4:-3])
    else:
        x_ref, xs_ref = refs[:2]
        p_ref = ps_ref = None
        w = list(refs[2:-3])
    for k in FFN_ROW_PARAMS:
        if k < len(w):
            w[k] = _RowOf(w[k], layer)
    o_ref, os_ref, act_ref = refs[-3:]
    i = pl.program_id(0)

    @pl.when(i < n_prompt)
    def _():
        _ffn_rows(x_ref, p_ref, w, o_ref, act_ref)

    @pl.when(i == n_prompt)
    def _():
        _ffn_rows(xs_ref, ps_ref, w, os_ref, act_ref)


def _ffn(x, xs, ffn_w, layer, p=None, ps=None, embed_w=None):
    n, d = x.shape
    ns = xs.shape[0]
    d_ff = _shape(ffn_w[2])[1]
    with_embed = embed_w is not None
    weights = list(ffn_w) + (list(embed_w) if with_embed else [])
    fixed = sum(_bytes(a) for a in weights) + 3 * _bytes(xs)
    per_row = 5 * _nbytes((d,), F32) + _nbytes((d_ff,), BF16)
    if with_embed:
        fixed += _bytes(ps)
        per_row += 2 * _nbytes((_shape(p)[1],), F32)
    tile = _pick_tile(n, fixed, per_row)
    assert ns <= tile and d_ff % FF_COLS == 0
    blocks = fixed + tile * per_row
    n_prompt = n // tile
    row = _rows(x, tile, n_prompt)
    args = [x, xs]
    in_specs = [row, _resident(xs)]
    if with_embed:
        args += [p, ps]
        in_specs += [_rows(p, tile, n_prompt), _resident(ps)]
    return pl.pallas_call(
        functools.partial(_ffn_body, with_embed=with_embed, n_prompt=n_prompt, layer=layer),
        grid=(n_prompt + 1,),
        in_specs=in_specs + [_resident(a) for a in weights],
        out_specs=[row, _whole_out((ns, d))],
        out_shape=[jax.ShapeDtypeStruct((n, d), F32), jax.ShapeDtypeStruct((ns, d), F32)],
        scratch_shapes=[pltpu.VMEM((tile, d_ff), BF16)],
        compiler_params=pltpu.CompilerParams(
            dimension_semantics=("arbitrary",), vmem_limit_bytes=_vmem_limit(blocks)),
        name="ffn_embed" if with_embed else "ffn",
    )(*[_arr(a) for a in args + weights])


def _mixer_prompt_rows(t, x_ref, pre_ref, post_ref, win_ref, wout_ref, acw_ref, lng_ref, lnb_ref,
                       ws_ref, bias_ref, ccw_ref, ccb_ref, clg_ref, clb_ref,
                       o_ref, na_ref, nc_ref, fa_ref, fc_ref, mix_ref):
    tt = x_ref.shape[0]
    aw = acw_ref.shape[1]
    bw = ws_ref.shape[0] * ws_ref.shape[1]
    hd = ws_ref.shape[1]
    cw = ccw_ref.shape[1]

    @pl.when(t == 0)
    def _():
        fa_ref[:, 0:A_HIST_PAD, :] = jnp.zeros((fa_ref.shape[0], A_HIST_PAD, V7X_LANES), F32)
        fc_ref[:, 0:C_HIST_PAD, :] = jnp.zeros((fc_ref.shape[0], C_HIST_PAD, V7X_LANES), F32)

    sub = min(tt, MIXER_SUB_ROWS)
    subs = [slice(r0, r0 + sub) for r0 in range(0, tt, sub)]
    hs = [_rms(x_ref[r, :], pre_ref[...]).astype(BF16) for r in subs]
    o_b = 3 * aw
    o_c = o_b + 2 * bw

    for r, h in zip(subs, hs):
        zc = _dot(h, win_ref[:, o_c:o_c + 2 * cw])
        _slab_store(fc_ref, C_HIST_PAD + r.start, zc[:, 0:cw] * jax.nn.sigmoid(zc[:, cw:2 * cw]))
    base = C_HIST_PAD - (C_KERNEL - 1)
    for r0 in range(0, tt, CONV_ROWS):
        acc = _causal_conv(fc_ref, ccw_ref, base + r0, CONV_ROWS) + ccb_ref[...]
        y_c = jax.nn.silu(_ln(acc, clg_ref[...], clb_ref[...]))
        mix_ref[r0:r0 + CONV_ROWS, aw + bw:aw + bw + cw] = y_c.astype(BF16)
    last_c = _slab_load(fc_ref, base + tt, C_KERNEL - 1)
    nc_ref[...] = last_c
    _slab_store(fc_ref, base, last_c)

    base = A_HIST_PAD - (A_KERNEL - 1)
    for r, h in zip(subs, hs):
        za = _dot(h, win_ref[:, 0:3 * aw])
        _slab_store(fa_ref, A_HIST_PAD + r.start, za[:, aw:2 * aw] * za[:, 0:aw])
        for q0 in range(0, sub, CONV_ROWS):
            r0 = r.start + q0
            y_a = za[q0:q0 + CONV_ROWS, 2 * aw:3 * aw] * _causal_conv(fa_ref, acw_ref, base + r0, CONV_ROWS)
            mix_ref[r0:r0 + CONV_ROWS, 0:aw] = y_a.astype(BF16)
    last_a = _slab_load(fa_ref, base + tt, A_KERNEL - 1)
    na_ref[...] = last_a
    _slab_store(fa_ref, base, last_a)

    row_i = lax.broadcasted_iota(jnp.int32, (CHUNK, CHUNK), 0)
    col_j = lax.broadcasted_iota(jnp.int32, (CHUNK, CHUNK), 1)
    w_trils = [jnp.where(col_j <= row_i, ws_ref[hh], 0.0).astype(BF16) for hh in range(B_HEADS)]
    n_chunks = sub // CHUNK
    us = [_gelu(_dot(h, win_ref[:, o_b:o_b + bw])) for h in hs]
    gvs = [_gelu(_dot(h, win_ref[:, o_b + bw:o_b + 2 * bw])) for h in hs]
    for r, u, gv in zip(subs, us, gvs):
        for hh in range(B_HEADS):
            cols = slice(hh * hd, (hh + 1) * hd)
            v = _ln(gv[:, cols], lng_ref[hh:hh + 1, :], lnb_ref[hh:hh + 1, :]).astype(BF16)
            bias = bias_ref[:, hh:hh + 1]
            v_cat = jnp.concatenate(
                [v[c * CHUNK:(c + 1) * CHUNK, :] for c in range(n_chunks)], axis=1)
            sp = _dot(w_trils[hh], v_cat)
            for c in range(n_chunks):
                y_b = u[c * CHUNK:(c + 1) * CHUNK, cols] * (sp[:, c * hd:(c + 1) * hd] + bias)
                rows = slice(r.start + c * CHUNK, r.start + (c + 1) * CHUNK)
                mix_ref[rows, aw + hh * hd:aw + (hh + 1) * hd] = y_b.astype(BF16)

    mixes = [_dot(mix_ref[r, :], wout_ref[...]) for r in subs]
    for r, mix in zip(subs, mixes):
        o_ref[r, :] = x_ref[r, :] + _rms(mix, post_ref[...])


def _state_conv(state_ref, w_ref, new):
    k1 = state_ref.shape[0]
    acc = w_ref[k1:k1 + 1, :] * new
    for k in range(k1):
        acc = acc + w_ref[k:k + 1, :] * state_ref[k]
    return acc


def _state_shift(out_ref, state_ref, new):
    k1 = state_ref.shape[0]
    out_ref[0:k1 - 1] = state_ref[1:k1]
    out_ref[k1 - 1] = new


def _mixer_sample_rows(x_ref, sa_ref, sc_ref, pre_ref, post_ref, win_ref, wout_ref, acw_ref,
                       lng_ref, lnb_ref, w00_ref, b0_ref, ccw_ref, ccb_ref, clg_ref, clb_ref,
                       o_ref, na_ref, nc_ref, v_ref, mix_ref):
    n = x_ref.shape[0]
    aw = acw_ref.shape[1]
    bw = w00_ref.shape[1]
    hd = bw // B_HEADS
    cw = ccw_ref.shape[1]
    rows = slice(0, n)
    x = x_ref[...]
    h = _rms(x, pre_ref[...]).astype(BF16)

    za = _dot(h, win_ref[:, 0:3 * aw])
    fa_new = za[:, aw:2 * aw] * za[:, 0:aw]
    conv_a = _state_conv(sa_ref, acw_ref, fa_new)
    mix_ref[rows, 0:aw] = (za[:, 2 * aw:3 * aw] * conv_a).astype(BF16)
    _state_shift(na_ref, sa_ref, fa_new)

    o_b = 3 * aw
    u = _gelu(_dot(h, win_ref[:, o_b:o_b + bw]))
    gv = _gelu(_dot(h, win_ref[:, o_b + bw:o_b + 2 * bw]))
    for hh in range(B_HEADS):
        cols = slice(hh * hd, (hh + 1) * hd)
        v = _ln(gv[:, cols], lng_ref[hh:hh + 1, :], lnb_ref[hh:hh + 1, :])
        v_ref[:, cols] = v
        sp = w00_ref[:, cols] * v + b0_ref[:, cols]
        mix_ref[rows, aw + hh * hd:aw + (hh + 1) * hd] = (u[:, cols] * sp).astype(BF16)

    o_c = o_b + 2 * bw
    zc = _dot(h, win_ref[:, o_c:o_c + 2 * cw])
    fc_new = zc[:, 0:cw] * jax.nn.sigmoid(zc[:, cw:2 * cw])
    acc = _state_conv(sc_ref, ccw_ref, fc_new) + ccb_ref[...]
    y_c = jax.nn.silu(_ln(acc, clg_ref[...], clb_ref[...]))
    mix_ref[rows, aw + bw:aw + bw + cw] = y_c.astype(BF16)
    _state_shift(nc_ref, sc_ref, fc_new)

    mix = _dot(mix_ref[rows, :], wout_ref[...])
    o_ref[...] = x + _rms(mix, post_ref[...])


def _mixer_body(x_ref, xs_ref, sa_ref, sc_ref, pre_ref, post_ref, win32_ref, wout32_ref,
                acw_ref, lng_ref, lnb_ref, ws_ref, bias_ref, w00_ref, b0_ref,
                ccw_ref, ccb_ref, clg_ref, clb_ref,
                o_ref, nap_ref, ncp_ref, os_ref, nas_ref, ncs_ref, v_ref,
                fa_ref, fc_ref, mix_ref, win_ref, wout_ref, *, n_prompt, tiles_per_seq, layer):
    s = pl.program_id(0)
    pre_ref, post_ref, ccb_ref, clg_ref, clb_ref = (
        _RowOf(r, layer) for r in (pre_ref, post_ref, ccb_ref, clg_ref, clb_ref))

    @pl.when(s == 0)
    def _():
        win_ref[...] = win32_ref[...].astype(BF16)
        wout_ref[...] = wout32_ref[...].astype(BF16)

    @pl.when(s < n_prompt)
    def _():
        _mixer_prompt_rows(lax.rem(s, tiles_per_seq), x_ref, pre_ref, post_ref, win_ref, wout_ref,
                           acw_ref, lng_ref, lnb_ref, ws_ref, bias_ref, ccw_ref, ccb_ref,
                           clg_ref, clb_ref, o_ref, nap_ref, ncp_ref, fa_ref, fc_ref, mix_ref)

    @pl.when(s == n_prompt)
    def _():
        _mixer_sample_rows(xs_ref, sa_ref, sc_ref, pre_ref, post_ref, win_ref, wout_ref, acw_ref,
                           lng_ref, lnb_ref, w00_ref, b0_ref, ccw_ref, ccb_ref, clg_ref, clb_ref,
                           os_ref, nas_ref, ncs_ref, v_ref, mix_ref)


def _mixer(x, xs, sa, sc, pre, post, w_in, w_out, a_cw, lng, lnb, ws, bias_t, w00, b0,
           c_cw, c_cb, clg, clb, layer):
    b, s, d = x.shape
    ns = xs.shape[0]
    d_in = _shape(w_in)[1]
    d_mix = _shape(w_out)[0]
    aw, cw = _shape(a_cw)[1], _shape(c_cw)[1]
    bw = _shape(w00)[1]
    assert _shape(ws)[1] == CHUNK and _shape(ws)[2] == CHUNK and CHUNK % CONV_ROWS == 0
    assert aw % V7X_LANES == 0 and cw % V7X_LANES == 0
    args = [x, xs, sa, sc, pre, post, w_in, w_out, a_cw, lng, lnb, ws, bias_t, w00, b0,
            c_cw, c_cb, clg, clb]
    sample_out = [(ns, d), _shape(sa), _shape(sc), (ns, bw)]
    fixed = (sum(_bytes(a) for a in args[1:]) + 2 * sum(_nbytes(o, F32) for o in sample_out)
             + _nbytes((d, d_in), BF16) + _nbytes((d_mix, d), BF16))
    per_row = 4 * _nbytes((d,), F32) + _nbytes((d_mix,), BF16) + _nbytes((aw + cw,), F32)
    tile = _pick_tile(s, fixed, per_row, multiple_of=CHUNK)
    assert ns <= tile
    blocks = fixed + tile * per_row
    tiles_per_seq = s // tile
    n_prompt = b * tiles_per_seq
    last = n_prompt - 1

    def seq_of(i):
        return lax.div(jnp.minimum(i, last), tiles_per_seq)

    def tile_of(i):
        return lax.rem(jnp.minimum(i, last), tiles_per_seq)

    xblk = pl.BlockSpec((None, tile, d), lambda i: (seq_of(i), tile_of(i), 0))
    return pl.pallas_call(
        functools.partial(_mixer_body, n_prompt=n_prompt, tiles_per_seq=tiles_per_seq,
                          layer=layer),
        grid=(n_prompt + 1,),
        in_specs=[xblk] + [_resident(a) for a in args[1:]],
        out_specs=[xblk,
                   pl.BlockSpec((None, A_KERNEL - 1, aw), lambda i: (seq_of(i), 0, 0)),
                   pl.BlockSpec((None, C_KERNEL - 1, cw), lambda i: (seq_of(i), 0, 0))]
                  + [_whole_out(o) for o in sample_out],
        out_shape=[jax.ShapeDtypeStruct((b, s, d), F32),
                   jax.ShapeDtypeStruct((b, A_KERNEL - 1, aw), F32),
                   jax.ShapeDtypeStruct((b, C_KERNEL - 1, cw), F32)]
                  + [jax.ShapeDtypeStruct(o, F32) for o in sample_out],
        scratch_shapes=[pltpu.VMEM((aw // V7X_LANES, A_HIST_PAD + tile, V7X_LANES), F32),
                        pltpu.VMEM((cw // V7X_LANES, C_HIST_PAD + tile, V7X_LANES), F32),
                        pltpu.VMEM((tile, d_mix), BF16),
                        pltpu.VMEM((d, d_in), BF16), pltpu.VMEM((d_mix, d), BF16)],
        compiler_params=pltpu.CompilerParams(
            dimension_semantics=("arbitrary",), vmem_limit_bytes=_vmem_limit(blocks)),
        name="mixer",
    )(*[_arr(a) for a in args])


def kernel(x_prompt, x_sample, state_conv_a, state_conv_c, p_prompt, p_sample, f1_pre, f1_post, f1_wg, f1_wu, f1_wd, m_pre, m_post, w_in, w_out, a_conv_w, b_ln_g, b_ln_b, b_ws, b_bias, c_conv_w, c_conv_b, c_ln_g, c_ln_b, f2_pre, f2_post, f2_wg, f2_wu, f2_wd, e_pre, e_post, e_wg, e_wp):
    depth = f1_wg.shape[0]
    nb, seq, d = x_prompt.shape
    ns = x_sample.shape[0]
    assert x_sample.shape[1] == 1
    hd = b_ws.shape[2]
    bw = B_HEADS * hd

    yp = x_prompt.reshape(nb * seq, d)
    ys = x_sample.reshape(ns, d)
    sa_t = jnp.transpose(state_conv_a, (0, 2, 1, 3))
    sc_t = jnp.transpose(state_conv_c, (0, 2, 1, 3))
    bias_t = jnp.transpose(b_bias, (0, 2, 1))
    w00 = jnp.repeat(b_ws[:, :, 0, 0], hd, axis=1).reshape(depth, 1, bw)
    b0 = jnp.repeat(b_bias[:, :, 0], hd, axis=1).reshape(depth, 1, bw)
    pp = p_prompt.reshape(depth, nb * seq, -1)
    ps = p_sample.reshape(depth, ns, -1)
    a_p, c_p, a_s, c_s, v_s = [], [], [], [], []
    for i in range(depth):
        row = lambda g: _Row(g, i)
        w = lambda a: _Layer(a, i)
        f1 = (row(f1_pre), row(f1_post), w(f1_wg), w(f1_wu), w(f1_wd))
        f2 = (row(f2_pre), row(f2_post), w(f2_wg), w(f2_wu), w(f2_wd))
        em = (row(e_pre), row(e_post), w(e_wg), w(e_wp))

        yp, ys = _ffn(yp, ys, f1, i)
        yp, na_p, nc_p, ys, na_s, nc_s, vr_s = _mixer(
            yp.reshape(nb, seq, d), ys, w(sa_t), w(sc_t),
            row(m_pre), row(m_post), w(w_in), w(w_out),
            w(a_conv_w), w(b_ln_g), w(b_ln_b), w(b_ws), w(bias_t), w(w00), w(b0),
            w(c_conv_w), row(c_conv_b), row(c_ln_g), row(c_ln_b), layer=i)
        yp, ys = _ffn(yp.reshape(nb * seq, d), ys, f2, i,
                      p=_Layer(pp, i), ps=_Layer(ps, i), embed_w=em)

        a_p.append(na_p)
        c_p.append(nc_p)
        a_s.append(na_s)
        c_s.append(nc_s)
        v_s.append(vr_s.reshape(ns, 1, bw))
    return (yp.reshape(nb, seq, d), ys.reshape(ns, 1, d), jnp.stack(a_p), jnp.stack(c_p),
            jnp.transpose(jnp.stack(a_s), (0, 2, 1, 3)),
            jnp.transpose(jnp.stack(c_s), (0, 2, 1, 3)), jnp.stack(v_s))
```

```python
import functools
import math
from typing import NamedTuple

import jax
import jax.numpy as jnp
from jax import lax
from jax.experimental import pallas as pl
from jax.experimental.pallas import tpu as pltpu

EPS = 1e-6
CHUNK = 128
A_KERNEL = 3
C_KERNEL = 31
B_HEADS = 4

V7X_SUBLANES = 8
V7X_LANES = 128
V7X_MXU_COLS = 256
V7X_VMEM_BYTES = 64 * 1024 * 1024

TILE_CANDIDATES = (1024, 512, 256, 128)
VMEM_HEADROOM = 5 << 20
FFN_SUB_ROWS = 512
FFN_EMBED_SUB_ROWS = 256
FFN_ROW_PARAMS = (0, 1, 5, 6)
MIXER_SUB_ROWS = 1024
FF_COLS = V7X_MXU_COLS
CONV_ROWS = 64
A_HIST_PAD = V7X_SUBLANES
C_HIST_PAD = 4 * V7X_SUBLANES

GELU_C = math.sqrt(2.0 / math.pi)
GELU_CUBIC = 0.044715

F32 = jnp.float32
BF16 = jnp.bfloat16


def _vmem_limit(block_bytes):
    return min(block_bytes + 2 * VMEM_HEADROOM, V7X_VMEM_BYTES - (2 << 20))


def _pick_tile(n_rows, fixed_bytes, row_bytes, multiple_of=1):
    for tile in TILE_CANDIDATES:
        fits = fixed_bytes + tile * row_bytes <= V7X_VMEM_BYTES - VMEM_HEADROOM
        if n_rows % tile == 0 and tile % multiple_of == 0 and fits:
            return tile
    raise ValueError("no prompt tile fits VMEM")


def _rms(x, g):
    return x * lax.rsqrt(jnp.mean(x * x, axis=-1, keepdims=True) + EPS) * g


def _ln(x, g, b):
    mu = jnp.mean(x, axis=-1, keepdims=True)
    xc = x - mu
    var = jnp.mean(xc * xc, axis=-1, keepdims=True)
    return xc * lax.rsqrt(var + EPS) * g + b


def _gelu(x):
    inner = x * (GELU_C + (GELU_C * GELU_CUBIC) * (x * x))
    hx = 0.5 * x
    return hx + hx * jnp.tanh(inner)


def _dot(a, b):
    return jnp.dot(a, b, preferred_element_type=F32)


class _Layer(NamedTuple):
    arr: jax.Array
    layer: int


class _Row(NamedTuple):
    arr: jax.Array
    layer: int


class _RowOf:
    def __init__(self, ref, layer):
        self.ref, self.layer = ref, layer

    def __getitem__(self, idx):
        assert idx is Ellipsis
        return self.ref[self.layer:self.layer + 1, :]


def _shape(a):
    if isinstance(a, _Layer):
        return a.arr.shape[1:]
    return a.arr.shape if isinstance(a, _Row) else a.shape


def _arr(a):
    return a.arr if isinstance(a, (_Layer, _Row)) else a


def _nbytes(shape, dtype):
    n = 1
    for s in shape:
        n *= s
    return n * jnp.dtype(dtype).itemsize


def _bytes(a):
    return _nbytes(_shape(a), _arr(a).dtype)


def _resident(a):
    shape = _shape(a)
    zeros = (0,) * len(shape)
    if isinstance(a, _Layer):
        return pl.BlockSpec((None,) + shape, lambda *_: (a.layer,) + zeros,
                            pipeline_mode=pl.Buffered(1))
    return pl.BlockSpec(shape, lambda *_: zeros, pipeline_mode=pl.Buffered(1))


def _whole_out(shape):
    zeros = (0,) * len(shape)
    return pl.BlockSpec(shape, lambda *_: zeros)


def _rows(a, tile, n_tiles):
    cols = _shape(a)[1]
    last = n_tiles - 1
    if isinstance(a, _Layer):
        return pl.BlockSpec((None, tile, cols), lambda i: (a.layer, jnp.minimum(i, last), 0))
    return pl.BlockSpec((tile, cols), lambda i: (jnp.minimum(i, last), 0))


def _slab_store(ref, row0, val):
    for j in range(ref.shape[0]):
        ref[j, row0:row0 + val.shape[0], :] = val[:, j * V7X_LANES:(j + 1) * V7X_LANES]


def _slab_load(ref, row0, rows):
    return jnp.concatenate([ref[j, row0:row0 + rows, :] for j in range(ref.shape[0])], axis=1)


def _causal_conv(ref, w_ref, row0, rows):
    out = []
    for j in range(ref.shape[0]):
        lanes = slice(j * V7X_LANES, (j + 1) * V7X_LANES)
        acc = w_ref[0:1, lanes] * ref[j, row0:row0 + rows, :]
        for k in range(1, w_ref.shape[0]):
            acc = acc + w_ref[k:k + 1, lanes] * ref[j, row0 + k:row0 + k + rows, :]
        out.append(acc)
    return jnp.concatenate(out, axis=1)


def _ffn_rows(x_ref, p_ref, w, o_ref, act_ref):
    pre_ref, post_ref, wg_ref, wu_ref, wd_ref = w[:5]
    d_ff = wg_ref.shape[1]
    n = x_ref.shape[0]
    sub = min(n, FFN_SUB_ROWS if p_ref is None else FFN_EMBED_SUB_ROWS)
    subs = [slice(r0, r0 + sub) for r0 in range(0, n, sub)]
    if p_ref is not None:
        epre_ref, epost_ref, ewg_ref, ewp_ref = w[5:]
        pes = [_dot(p_ref[r, :].astype(BF16), ewp_ref[...]) for r in subs]
    xs = [x_ref[r, :] for r in subs]
    hs = [_rms(x, pre_ref[...]).astype(BF16) for x in xs]
    for r, h in zip(subs, hs):
        for c0 in range(0, d_ff, FF_COLS):
            g = _dot(h, wg_ref[:, c0:c0 + FF_COLS])
            u = _dot(h, wu_ref[:, c0:c0 + FF_COLS])
            act_ref[r, c0:c0 + FF_COLS] = (jax.nn.silu(g) * u).astype(BF16)
    ys = [_dot(act_ref[r, :], wd_ref[...]) for r in subs]
    xs = [x + 0.5 * _rms(y, post_ref[...]) for x, y in zip(xs, ys)]
    if p_ref is not None:
        hs = [_rms(x, epre_ref[...]).astype(BF16) for x in xs]
        gates = [jax.nn.sigmoid(_dot(h, ewg_ref[...])) for h in hs]
        xs = [x + _rms(gate * pe, epost_ref[...]) for x, gate, pe in zip(xs, gates, pes)]
    for r, x in zip(subs, xs):
        o_ref[r, :] = x


def _ffn_body(*refs, with_embed, n_prompt, layer):
    if with_embed:
        x_ref, xs_ref, p_ref, ps_ref = refs[:4]
        w = list(refs[4:-3])
    else:
        x_ref, xs_ref = refs[:2]
        p_ref = ps_ref = None
        w = list(refs[2:-3])
    for k in FFN_ROW_PARAMS:
        if k < len(w):
            w[k] = _RowOf(w[k], layer)
    o_ref, os_ref, act_ref = refs[-3:]
    i = pl.program_id(0)

    @pl.when(i < n_prompt)
    def _():
        _ffn_rows(x_ref, p_ref, w, o_ref, act_ref)

    @pl.when(i == n_prompt)
    def _():
        _ffn_rows(xs_ref, ps_ref, w, os_ref, act_ref)


def _ffn(x, xs, ffn_w, layer, p=None, ps=None, embed_w=None):
    n, d = x.shape
    ns = xs.shape[0]
    d_ff = _shape(ffn_w[2])[1]
    with_embed = embed_w is not None
    weights = list(ffn_w) + (list(embed_w) if with_embed else [])
    fixed = sum(_bytes(a) for a in weights) + 3 * _bytes(xs)
    per_row = 5 * _nbytes((d,), F32) + _nbytes((d_ff,), BF16)
    if with_embed:
        fixed += _bytes(ps)
        per_row += 2 * _nbytes((_shape(p)[1],), F32)
    tile = _pick_tile(n, fixed, per_row)
    assert ns <= tile and d_ff % FF_COLS == 0
    blocks = fixed + tile * per_row
    n_prompt = n // tile
    row = _rows(x, tile, n_prompt)
    args = [x, xs]
    in_specs = [row, _resident(xs)]
    if with_embed:
        args += [p, ps]
        in_specs += [_rows(p, tile, n_prompt), _resident(ps)]
    return pl.pallas_call(
        functools.partial(_ffn_body, with_embed=with_embed, n_prompt=n_prompt, layer=layer),
        grid=(n_prompt + 1,),
        in_specs=in_specs + [_resident(a) for a in weights],
        out_specs=[row, _whole_out((ns, d))],
        out_shape=[jax.ShapeDtypeStruct((n, d), F32), jax.ShapeDtypeStruct((ns, d), F32)],
        scratch_shapes=[pltpu.VMEM((tile, d_ff), BF16)],
        compiler_params=pltpu.CompilerParams(
            dimension_semantics=("arbitrary",), vmem_limit_bytes=_vmem_limit(blocks)),
        name="ffn_embed" if with_embed else "ffn",
    )(*[_arr(a) for a in args + weights])


def _mixer_prompt_rows(t, x_ref, pre_ref, post_ref, win_ref, wout_ref, acw_ref, lng_ref, lnb_ref,
                       ws_ref, bias_ref, ccw_ref, ccb_ref, clg_ref, clb_ref,
                       o_ref, na_ref, nc_ref, fa_ref, fc_ref, mix_ref):
    tt = x_ref.shape[0]
    aw = acw_ref.shape[1]
    bw = ws_ref.shape[0] * ws_ref.shape[1]
    hd = ws_ref.shape[1]
    cw = ccw_ref.shape[1]

    @pl.when(t == 0)
    def _():
        fa_ref[:, 0:A_HIST_PAD, :] = jnp.zeros((fa_ref.shape[0], A_HIST_PAD, V7X_LANES), F32)
        fc_ref[:, 0:C_HIST_PAD, :] = jnp.zeros((fc_ref.shape[0], C_HIST_PAD, V7X_LANES), F32)

    sub = min(tt, MIXER_SUB_ROWS)
    subs = [slice(r0, r0 + sub) for r0 in range(0, tt, sub)]
    hs = [_rms(x_ref[r, :], pre_ref[...]).astype(BF16) for r in subs]
    o_b = 3 * aw
    o_c = o_b + 2 * bw

    for r, h in zip(subs, hs):
        zc = _dot(h, win_ref[:, o_c:o_c + 2 * cw])
        _slab_store(fc_ref, C_HIST_PAD + r.start, zc[:, 0:cw] * jax.nn.sigmoid(zc[:, cw:2 * cw]))
    base = C_HIST_PAD - (C_KERNEL - 1)
    for r0 in range(0, tt, CONV_ROWS):
        acc = _causal_conv(fc_ref, ccw_ref, base + r0, CONV_ROWS) + ccb_ref[...]
        y_c = jax.nn.silu(_ln(acc, clg_ref[...], clb_ref[...]))
        mix_ref[r0:r0 + CONV_ROWS, aw + bw:aw + bw + cw] = y_c.astype(BF16)
    last_c = _slab_load(fc_ref, base + tt, C_KERNEL - 1)
    nc_ref[...] = last_c
    _slab_store(fc_ref, base, last_c)

    base = A_HIST_PAD - (A_KERNEL - 1)
    for r, h in zip(subs, hs):
        za = _dot(h, win_ref[:, 0:3 * aw])
        _slab_store(fa_ref, A_HIST_PAD + r.start, za[:, aw:2 * aw] * za[:, 0:aw])
        for q0 in range(0, sub, CONV_ROWS):
            r0 = r.start + q0
            y_a = za[q0:q0 + CONV_ROWS, 2 * aw:3 * aw] * _causal_conv(fa_ref, acw_ref, base + r0, CONV_ROWS)
            mix_ref[r0:r0 + CONV_ROWS, 0:aw] = y_a.astype(BF16)
    last_a = _slab_load(fa_ref, base + tt, A_KERNEL - 1)
    na_ref[...] = last_a
    _slab_store(fa_ref, base, last_a)

    row_i = lax.broadcasted_iota(jnp.int32, (CHUNK, CHUNK), 0)
    col_j = lax.broadcasted_iota(jnp.int32, (CHUNK, CHUNK), 1)
    w_trils = [jnp.where(col_j <= row_i, ws_ref[hh], 0.0).astype(BF16) for hh in range(B_HEADS)]
    n_chunks = sub // CHUNK
    us = [_gelu(_dot(h, win_ref[:, o_b:o_b + bw])) for h in hs]
    gvs = [_gelu(_dot(h, win_ref[:, o_b + bw:o_b + 2 * bw])) for h in hs]
    for r, u, gv in zip(subs, us, gvs):
        for hh in range(B_HEADS):
            cols = slice(hh * hd, (hh + 1) * hd)
            v = _ln(gv[:, cols], lng_ref[hh:hh + 1, :], lnb_ref[hh:hh + 1, :]).astype(BF16)
            bias = bias_ref[:, hh:hh + 1]
            v_cat = jnp.concatenate(
                [v[c * CHUNK:(c + 1) * CHUNK, :] for c in range(n_chunks)], axis=1)
            sp = _dot(w_trils[hh], v_cat)
            for c in range(n_chunks):
                y_b = u[c * CHUNK:(c + 1) * CHUNK, cols] * (sp[:, c * hd:(c + 1) * hd] + bias)
                rows = slice(r.start + c * CHUNK, r.start + (c + 1) * CHUNK)
                mix_ref[rows, aw + hh * hd:aw + (hh + 1) * hd] = y_b.astype(BF16)

    mixes = [_dot(mix_ref[r, :], wout_ref[...]) for r in subs]
    for r, mix in zip(subs, mixes):
        o_ref[r, :] = x_ref[r, :] + _rms(mix, post_ref[...])


def _state_conv(state_ref, w_ref, new):
    k1 = state_ref.shape[0]
    acc = w_ref[k1:k1 + 1, :] * new
    for k in range(k1):
        acc = acc + w_ref[k:k + 1, :] * state_ref[k]
    return acc


def _state_shift(out_ref, state_ref, new):
    k1 = state_ref.shape[0]
    out_ref[0:k1 - 1] = state_ref[1:k1]
    out_ref[k1 - 1] = new


def _mixer_sample_rows(x_ref, sa_ref, sc_ref, pre_ref, post_ref, win_ref, wout_ref, acw_ref,
                       lng_ref, lnb_ref, w00_ref, b0_ref, ccw_ref, ccb_ref, clg_ref, clb_ref,
                       o_ref, na_ref, nc_ref, v_ref, mix_ref):
    n = x_ref.shape[0]
    aw = acw_ref.shape[1]
    bw = w00_ref.shape[1]
    hd = bw // B_HEADS
    cw = ccw_ref.shape[1]
    rows = slice(0, n)
    x = x_ref[...]
    h = _rms(x, pre_ref[...]).astype(BF16)

    za = _dot(h, win_ref[:, 0:3 * aw])
    fa_new = za[:, aw:2 * aw] * za[:, 0:aw]
    conv_a = _state_conv(sa_ref, acw_ref, fa_new)
    mix_ref[rows, 0:aw] = (za[:, 2 * aw:3 * aw] * conv_a).astype(BF16)
    _state_shift(na_ref, sa_ref, fa_new)

    o_b = 3 * aw
    u = _gelu(_dot(h, win_ref[:, o_b:o_b + bw]))
    gv = _gelu(_dot(h, win_ref[:, o_b + bw:o_b + 2 * bw]))
    for hh in range(B_HEADS):
        cols = slice(hh * hd, (hh + 1) * hd)
        v = _ln(gv[:, cols], lng_ref[hh:hh + 1, :], lnb_ref[hh:hh + 1, :])
        v_ref[:, cols] = v
        sp = w00_ref[:, cols] * v + b0_ref[:, cols]
        mix_ref[rows, aw + hh * hd:aw + (hh + 1) * hd] = (u[:, cols] * sp).astype(BF16)

    o_c = o_b + 2 * bw
    zc = _dot(h, win_ref[:, o_c:o_c + 2 * cw])
    fc_new = zc[:, 0:cw] * jax.nn.sigmoid(zc[:, cw:2 * cw])
    acc = _state_conv(sc_ref, ccw_ref, fc_new) + ccb_ref[...]
    y_c = jax.nn.silu(_ln(acc, clg_ref[...], clb_ref[...]))
    mix_ref[rows, aw + bw:aw + bw + cw] = y_c.astype(BF16)
    _state_shift(nc_ref, sc_ref, fc_new)

    mix = _dot(mix_ref[rows, :], wout_ref[...])
    o_ref[...] = x + _rms(mix, post_ref[...])


def _mixer_body(x_ref, xs_ref, sa_ref, sc_ref, pre_ref, post_ref, win32_ref, wout32_ref,
                acw_ref, lng_ref, lnb_ref, ws_ref, bias_ref, w00_ref, b0_ref,
                ccw_ref, ccb_ref, clg_ref, clb_ref,
                o_ref, nap_ref, ncp_ref, os_ref, nas_ref, ncs_ref, v_ref,
                fa_ref, fc_ref, mix_ref, win_ref, wout_ref, *, n_prompt, tiles_per_seq, layer):
    s = pl.program_id(0)
    pre_ref, post_ref, ccb_ref, clg_ref, clb_ref = (
        _RowOf(r, layer) for r in (pre_ref, post_ref, ccb_ref, clg_ref, clb_ref))

    @pl.when(s == 0)
    def _():
        win_ref[...] = win32_ref[...].astype(BF16)
        wout_ref[...] = wout32_ref[...].astype(BF16)

    @pl.when(s < n_prompt)
    def _():
        _mixer_prompt_rows(lax.rem(s, tiles_per_seq), x_ref, pre_ref, post_ref, win_ref, wout_ref,
                           acw_ref, lng_ref, lnb_ref, ws_ref, bias_ref, ccw_ref, ccb_ref,
                           clg_ref, clb_ref, o_ref, nap_ref, ncp_ref, fa_ref, fc_ref, mix_ref)

    @pl.when(s == n_prompt)
    def _():
        _mixer_sample_rows(xs_ref, sa_ref, sc_ref, pre_ref, post_ref, win_ref, wout_ref, acw_ref,
                           lng_ref, lnb_ref, w00_ref, b0_ref, ccw_ref, ccb_ref, clg_ref, clb_ref,
                           os_ref, nas_ref, ncs_ref, v_ref, mix_ref)


def _mixer(x, xs, sa, sc, pre, post, w_in, w_out, a_cw, lng, lnb, ws, bias_t, w00, b0,
           c_cw, c_cb, clg, clb, layer):
    b, s, d = x.shape
    ns = xs.shape[0]
    d_in = _shape(w_in)[1]
    d_mix = _shape(w_out)[0]
    aw, cw = _shape(a_cw)[1], _shape(c_cw)[1]
    bw = _shape(w00)[1]
    assert _shape(ws)[1] == CHUNK and _shape(ws)[2] == CHUNK and CHUNK % CONV_ROWS == 0
    assert aw % V7X_LANES == 0 and cw % V7X_LANES == 0
    args = [x, xs, sa, sc, pre, post, w_in, w_out, a_cw, lng, lnb, ws, bias_t, w00, b0,
            c_cw, c_cb, clg, clb]
    sample_out = [(ns, d), _shape(sa), _shape(sc), (ns, bw)]
    fixed = (sum(_bytes(a) for a in args[1:]) + 2 * sum(_nbytes(o, F32) for o in sample_out)
             + _nbytes((d, d_in), BF16) + _nbytes((d_mix, d), BF16))
    per_row = 4 * _nbytes((d,), F32) + _nbytes((d_mix,), BF16) + _nbytes((aw + cw,), F32)
    tile = _pick_tile(s, fixed, per_row, multiple_of=CHUNK)
    assert ns <= tile
    blocks = fixed + tile * per_row
    tiles_per_seq = s // tile
    n_prompt = b * tiles_per_seq
    last = n_prompt - 1

    def seq_of(i):
        return lax.div(jnp.minimum(i, last), tiles_per_seq)

    def tile_of(i):
        return lax.rem(jnp.minimum(i, last), tiles_per_seq)

    xblk = pl.BlockSpec((None, tile, d), lambda i: (seq_of(i), tile_of(i), 0))
    return pl.pallas_call(
        functools.partial(_mixer_body, n_prompt=n_prompt, tiles_per_seq=tiles_per_seq,
                          layer=layer),
        grid=(n_prompt + 1,),
        in_specs=[xblk] + [_resident(a) for a in args[1:]],
        out_specs=[xblk,
                   pl.BlockSpec((None, A_KERNEL - 1, aw), lambda i: (seq_of(i), 0, 0)),
                   pl.BlockSpec((None, C_KERNEL - 1, cw), lambda i: (seq_of(i), 0, 0))]
                  + [_whole_out(o) for o in sample_out],
        out_shape=[jax.ShapeDtypeStruct((b, s, d), F32),
                   jax.ShapeDtypeStruct((b, A_KERNEL - 1, aw), F32),
                   jax.ShapeDtypeStruct((b, C_KERNEL - 1, cw), F32)]
                  + [jax.ShapeDtypeStruct(o, F32) for o in sample_out],
        scratch_shapes=[pltpu.VMEM((aw // V7X_LANES, A_HIST_PAD + tile, V7X_LANES), F32),
                        pltpu.VMEM((cw // V7X_LANES, C_HIST_PAD + tile, V7X_LANES), F32),
                        pltpu.VMEM((tile, d_mix), BF16),
                        pltpu.VMEM((d, d_in), BF16), pltpu.VMEM((d_mix, d), BF16)],
        compiler_params=pltpu.CompilerParams(
            dimension_semantics=("arbitrary",), vmem_limit_bytes=_vmem_limit(blocks)),
        name="mixer",
    )(*[_arr(a) for a in args])


def kernel(x_prompt, x_sample, state_conv_a, state_conv_c, p_prompt, p_sample, f1_pre, f1_post, f1_wg, f1_wu, f1_wd, m_pre, m_post, w_in, w_out, a_conv_w, b_ln_g, b_ln_b, b_ws, b_bias, c_conv_w, c_conv_b, c_ln_g, c_ln_b, f2_pre, f2_post, f2_wg, f2_wu, f2_wd, e_pre, e_post, e_wg, e_wp):
    depth = f1_wg.shape[0]
    nb, seq, d = x_prompt.shape
    ns = x_sample.shape[0]
    assert x_sample.shape[1] == 1
    hd = b_ws.shape[2]
    bw = B_HEADS * hd

    yp = x_prompt.reshape(nb * seq, d)
    ys = x_sample.reshape(ns, d)
    sa_t = jnp.transpose(state_conv_a, (0, 2, 1, 3))
    sc_t = jnp.transpose(state_conv_c, (0, 2, 1, 3))
    bias_t = jnp.transpose(b_bias, (0, 2, 1))
    w00 = jnp.repeat(b_ws[:, :, 0, 0], hd, axis=1).reshape(depth, 1, bw)
    b0 = jnp.repeat(b_bias[:, :, 0], hd, axis=1).reshape(depth, 1, bw)
    pp = p_prompt.reshape(depth, nb * seq, -1)
    ps = p_sample.reshape(depth, ns, -1)
    a_p, c_p, a_s, c_s, v_s = [], [], [], [], []
    for i in range(depth):
        row = lambda g: _Row(g, i)
        w = lambda a: _Layer(a, i)
        f1 = (row(f1_pre), row(f1_post), w(f1_wg), w(f1_wu), w(f1_wd))
        f2 = (row(f2_pre), row(f2_post), w(f2_wg), w(f2_wu), w(f2_wd))
        em = (row(e_pre), row(e_post), w(e_wg), w(e_wp))

        yp, ys = _ffn(yp, ys, f1, i)
        yp, na_p, nc_p, ys, na_s, nc_s, vr_s = _mixer(
            yp.reshape(nb, seq, d), ys, w(sa_t), w(sc_t),
            row(m_pre), row(m_post), w(w_in), w(w_out),
            w(a_conv_w), w(b_ln_g), w(b_ln_b), w(b_ws), w(bias_t), w(w00), w(b0),
            w(c_conv_w), row(c_conv_b), row(c_ln_g), row(c_ln_b), layer=i)
        yp, ys = _ffn(yp.reshape(nb * seq, d), ys, f2, i,
                      p=_Layer(pp, i), ps=_Layer(ps, i), embed_w=em)

        a_p.append(na_p)
        c_p.append(nc_p)
        a_s.append(na_s)
        c_s.append(nc_s)
        v_s.append(vr_s.reshape(ns, 1, bw))
    return (yp.reshape(nb, seq, d), ys.reshape(ns, 1, d), jnp.stack(a_p), jnp.stack(c_p),
            jnp.transpose(jnp.stack(a_s), (0, 2, 1, 3)),
            jnp.transpose(jnp.stack(c_s), (0, 2, 1, 3)), jnp.stack(v_s))
```

```python
import functools
import math
from typing import NamedTuple

import jax
import jax.numpy as jnp
from jax import lax
from jax.experimental import pallas as pl
from jax.experimental.pallas import tpu as pltpu

EPS = 1e-6
CHUNK = 128
A_KERNEL = 3
C_KERNEL = 31
B_HEADS = 4

V7X_SUBLANES = 8
V7X_LANES = 128
V7X_MXU_COLS = 256
V7X_VMEM_BYTES = 64 * 1024 * 1024

TILE_CANDIDATES = (1024, 512, 256, 128)
VMEM_HEADROOM = 5 << 20
FFN_SUB_ROWS = 256
FFN_ROW_PARAMS = (0, 1, 5, 6)
MIXER_SUB_ROWS = 1024
FF_COLS = V7X_MXU_COLS
CONV_ROWS = 32
A_HIST_PAD = V7X_SUBLANES
C_HIST_PAD = 4 * V7X_SUBLANES

GELU_C = math.sqrt(2.0 / math.pi)
GELU_CUBIC = 0.044715

F32 = jnp.float32
BF16 = jnp.bfloat16


def _vmem_limit(block_bytes):
    return min(block_bytes + 2 * VMEM_HEADROOM, V7X_VMEM_BYTES - (2 << 20))


def _pick_tile(n_rows, fixed_bytes, row_bytes, multiple_of=1):
    for tile in TILE_CANDIDATES:
        fits = fixed_bytes + tile * row_bytes <= V7X_VMEM_BYTES - VMEM_HEADROOM
        if n_rows % tile == 0 and tile % multiple_of == 0 and fits:
            return tile
    raise ValueError("no prompt tile fits VMEM")


def _rms(x, g):
    return x * lax.rsqrt(jnp.mean(x * x, axis=-1, keepdims=True) + EPS) * g


def _ln(x, g, b):
    mu = jnp.mean(x, axis=-1, keepdims=True)
    xc = x - mu
    var = jnp.mean(xc * xc, axis=-1, keepdims=True)
    return xc * lax.rsqrt(var + EPS) * g + b


def _gelu(x):
    inner = x * (GELU_C + (GELU_C * GELU_CUBIC) * (x * x))
    hx = 0.5 * x
    return hx + hx * jnp.tanh(inner)


def _dot(a, b):
    return jnp.dot(a, b, preferred_element_type=F32)


class _Layer(NamedTuple):
    arr: jax.Array
    layer: int


class _Row(NamedTuple):
    arr: jax.Array
    layer: int


class _RowOf:
    def __init__(self, ref, layer):
        self.ref, self.layer = ref, layer

    def __getitem__(self, idx):
        assert idx is Ellipsis
        return self.ref[self.layer:self.layer + 1, :]


def _shape(a):
    if isinstance(a, _Layer):
        return a.arr.shape[1:]
    return a.arr.shape if isinstance(a, _Row) else a.shape


def _arr(a):
    return a.arr if isinstance(a, (_Layer, _Row)) else a


def _nbytes(shape, dtype):
    n = 1
    for s in shape:
        n *= s
    return n * jnp.dtype(dtype).itemsize


def _bytes(a):
    return _nbytes(_shape(a), _arr(a).dtype)


def _resident(a):
    shape = _shape(a)
    zeros = (0,) * len(shape)
    if isinstance(a, _Layer):
        return pl.BlockSpec((None,) + shape, lambda *_: (a.layer,) + zeros,
                            pipeline_mode=pl.Buffered(1))
    return pl.BlockSpec(shape, lambda *_: zeros, pipeline_mode=pl.Buffered(1))


def _whole_out(shape):
    zeros = (0,) * len(shape)
    return pl.BlockSpec(shape, lambda *_: zeros)


def _rows(a, tile, n_tiles):
    cols = _shape(a)[1]
    last = n_tiles - 1
    if isinstance(a, _Layer):
        return pl.BlockSpec((None, tile, cols), lambda i: (a.layer, jnp.minimum(i, last), 0))
    return pl.BlockSpec((tile, cols), lambda i: (jnp.minimum(i, last), 0))


def _slab_store(ref, row0, val):
    for j in range(ref.shape[0]):
        ref[j, row0:row0 + val.shape[0], :] = val[:, j * V7X_LANES:(j + 1) * V7X_LANES]


def _slab_load(ref, row0, rows):
    return jnp.concatenate([ref[j, row0:row0 + rows, :] for j in range(ref.shape[0])], axis=1)


def _causal_conv(ref, w_ref, row0, rows):
    out = []
    for j in range(ref.shape[0]):
        lanes = slice(j * V7X_LANES, (j + 1) * V7X_LANES)
        acc = w_ref[0:1, lanes] * ref[j, row0:row0 + rows, :]
        for k in range(1, w_ref.shape[0]):
            acc = acc + w_ref[k:k + 1, lanes] * ref[j, row0 + k:row0 + k + rows, :]
        out.append(acc)
    return jnp.concatenate(out, axis=1)


def _ffn_rows(x_ref, p_ref, w, o_ref, act_ref):
    pre_ref, post_ref, wg_ref, wu_ref, wd_ref = w[:5]
    d_ff = wg_ref.shape[1]
    n = x_ref.shape[0]
    sub = min(n, FFN_SUB_ROWS)
    subs = [slice(r0, r0 + sub) for r0 in range(0, n, sub)]
    if p_ref is not None:
        epre_ref, epost_ref, ewg_ref, ewp_ref = w[5:]
        pes = [_dot(p_ref[r, :].astype(BF16), ewp_ref[...]) for r in subs]
    xs = [x_ref[r, :] for r in subs]
    hs = [_rms(x, pre_ref[...]).astype(BF16) for x in xs]
    for r, h in zip(subs, hs):
        for c0 in range(0, d_ff, FF_COLS):
            g = _dot(h, wg_ref[:, c0:c0 + FF_COLS])
            u = _dot(h, wu_ref[:, c0:c0 + FF_COLS])
            act_ref[r, c0:c0 + FF_COLS] = (jax.nn.silu(g) * u).astype(BF16)
    ys = [_dot(act_ref[r, :], wd_ref[...]) for r in subs]
    xs = [x + 0.5 * _rms(y, post_ref[...]) for x, y in zip(xs, ys)]
    if p_ref is not None:
        hs = [_rms(x, epre_ref[...]).astype(BF16) for x in xs]
        gates = [jax.nn.sigmoid(_dot(h, ewg_ref[...])) for h in hs]
        xs = [x + _rms(gate * pe, epost_ref[...]) for x, gate, pe in zip(xs, gates, pes)]
    for r, x in zip(subs, xs):
        o_ref[r, :] = x


def _ffn_body(*refs, with_embed, n_prompt, layer):
    if with_embed:
        x_ref, xs_ref, p_ref, ps_ref = refs[:4]
        w = list(refs[4:-3])
    else:
        x_ref, xs_ref = refs[:2]
        p_ref = ps_ref = None
        w = list(refs[2:-3])
    for k in FFN_ROW_PARAMS:
        if k < len(w):
            w[k] = _RowOf(w[k], layer)
    o_ref, os_ref, act_ref = refs[-3:]
    i = pl.program_id(0)

    @pl.when(i < n_prompt)
    def _():
        _ffn_rows(x_ref, p_ref, w, o_ref, act_ref)

    @pl.when(i == n_prompt)
    def _():
        _ffn_rows(xs_ref, ps_ref, w, os_ref, act_ref)


def _ffn(x, xs, ffn_w, layer, p=None, ps=None, embed_w=None):
    n, d = x.shape
    ns = xs.shape[0]
    d_ff = _shape(ffn_w[2])[1]
    with_embed = embed_w is not None
    weights = list(ffn_w) + (list(embed_w) if with_embed else [])
    fixed = sum(_bytes(a) for a in weights) + 3 * _bytes(xs)
    per_row = 5 * _nbytes((d,), F32) + _nbytes((d_ff,), BF16)
    if with_embed:
        fixed += _bytes(ps)
        per_row += 2 * _nbytes((_shape(p)[1],), F32)
    tile = _pick_tile(n, fixed, per_row)
    assert ns <= tile and d_ff % FF_COLS == 0
    blocks = fixed + tile * per_row
    n_prompt = n // tile
    row = _rows(x, tile, n_prompt)
    args = [x, xs]
    in_specs = [row, _resident(xs)]
    if with_embed:
        args += [p, ps]
        in_specs += [_rows(p, tile, n_prompt), _resident(ps)]
    return pl.pallas_call(
        functools.partial(_ffn_body, with_embed=with_embed, n_prompt=n_prompt, layer=layer),
        grid=(n_prompt + 1,),
        in_specs=in_specs + [_resident(a) for a in weights],
        out_specs=[row, _whole_out((ns, d))],
        out_shape=[jax.ShapeDtypeStruct((n, d), F32), jax.ShapeDtypeStruct((ns, d), F32)],
        scratch_shapes=[pltpu.VMEM((tile, d_ff), BF16)],
        compiler_params=pltpu.CompilerParams(
            dimension_semantics=("arbitrary",), vmem_limit_bytes=_vmem_limit(blocks)),
        name="ffn_embed" if with_embed else "ffn",
    )(*[_arr(a) for a in args + weights])


def _mixer_prompt_rows(t, x_ref, pre_ref, post_ref, win_ref, wout_ref, acw_ref, lng_ref, lnb_ref,
                       ws_ref, bias_ref, ccw_ref, ccb_ref, clg_ref, clb_ref,
                       o_ref, na_ref, nc_ref, fa_ref, fc_ref, mix_ref):
    tt = x_ref.shape[0]
    aw = acw_ref.shape[1]
    bw = ws_ref.shape[0] * ws_ref.shape[1]
    hd = ws_ref.shape[1]
    cw = ccw_ref.shape[1]

    @pl.when(t == 0)
    def _():
        fa_ref[:, 0:A_HIST_PAD, :] = jnp.zeros((fa_ref.shape[0], A_HIST_PAD, V7X_LANES), F32)
        fc_ref[:, 0:C_HIST_PAD, :] = jnp.zeros((fc_ref.shape[0], C_HIST_PAD, V7X_LANES), F32)

    sub = min(tt, MIXER_SUB_ROWS)
    subs = [slice(r0, r0 + sub) for r0 in range(0, tt, sub)]
    hs = [_rms(x_ref[r, :], pre_ref[...]).astype(BF16) for r in subs]
    o_b = 3 * aw
    o_c = o_b + 2 * bw

    for r, h in zip(subs, hs):
        zc = _dot(h, win_ref[:, o_c:o_c + 2 * cw])
        _slab_store(fc_ref, C_HIST_PAD + r.start, zc[:, 0:cw] * jax.nn.sigmoid(zc[:, cw:2 * cw]))
    base = C_HIST_PAD - (C_KERNEL - 1)
    n_blocks = tt // CONV_ROWS
    anchors = {n_blocks // 8: 0, 3 * n_blocks // 8: o_b, 5 * n_blocks // 8: o_b + bw}
    for blk, r0 in enumerate(range(0, tt, CONV_ROWS)):
        acc = _causal_conv(fc_ref, ccw_ref, base + r0, CONV_ROWS) + ccb_ref[...]
        y_c = jax.nn.silu(_ln(acc, clg_ref[...], clb_ref[...]))
        mix_ref[r0:r0 + CONV_ROWS, aw + bw:aw + bw + cw] = y_c.astype(BF16)
        if blk in anchors:
            _pin_after(win_ref, anchors[blk], y_c)
    last_c = _slab_load(fc_ref, base + tt, C_KERNEL - 1)
    nc_ref[...] = last_c
    _slab_store(fc_ref, base, last_c)

    base = A_HIST_PAD - (A_KERNEL - 1)
    for r, h in zip(subs, hs):
        za = _dot(h, win_ref[:, 0:3 * aw])
        _slab_store(fa_ref, A_HIST_PAD + r.start, za[:, aw:2 * aw] * za[:, 0:aw])
        for q0 in range(0, sub, CONV_ROWS):
            r0 = r.start + q0
            y_a = za[q0:q0 + CONV_ROWS, 2 * aw:3 * aw] * _causal_conv(fa_ref, acw_ref, base + r0, CONV_ROWS)
            mix_ref[r0:r0 + CONV_ROWS, 0:aw] = y_a.astype(BF16)
    last_a = _slab_load(fa_ref, base + tt, A_KERNEL - 1)
    na_ref[...] = last_a
    _slab_store(fa_ref, base, last_a)

    row_i = lax.broadcasted_iota(jnp.int32, (CHUNK, CHUNK), 0)
    col_j = lax.broadcasted_iota(jnp.int32, (CHUNK, CHUNK), 1)
    w_trils = [jnp.where(col_j <= row_i, ws_ref[hh], 0.0).astype(BF16) for hh in range(B_HEADS)]
    n_chunks = sub // CHUNK
    us = [_gelu(_dot(h, win_ref[:, o_b:o_b + bw])) for h in hs]
    gvs = [_gelu(_dot(h, win_ref[:, o_b + bw:o_b + 2 * bw])) for h in hs]
    for r, u, gv in zip(subs, us, gvs):
        for hh in range(B_HEADS):
            cols = slice(hh * hd, (hh + 1) * hd)
            v = _ln(gv[:, cols], lng_ref[hh:hh + 1, :], lnb_ref[hh:hh + 1, :]).astype(BF16)
            bias = bias_ref[:, hh:hh + 1]
            v_cat = jnp.concatenate(
                [v[c * CHUNK:(c + 1) * CHUNK, :] for c in range(n_chunks)], axis=1)
            sp = _dot(w_trils[hh], v_cat)
            for c in range(n_chunks):
                y_b = u[c * CHUNK:(c + 1) * CHUNK, cols] * (sp[:, c * hd:(c + 1) * hd] + bias)
                rows = slice(r.start + c * CHUNK, r.start + (c + 1) * CHUNK)
                mix_ref[rows, aw + hh * hd:aw + (hh + 1) * hd] = y_b.astype(BF16)

    mixes = [_dot(mix_ref[r, :], wout_ref[...]) for r in subs]
    for r, mix in zip(subs, mixes):
        o_ref[r, :] = x_ref[r, :] + _rms(mix, post_ref[...])


def _pin_after(w_ref, col0, val):
    rows, lanes = 2 * V7X_SUBLANES, V7X_LANES
    bits = pltpu.bitcast(val[0:rows, 0:lanes], jnp.uint32)
    half = jnp.uint32(16)
    zero = lax.shift_right_logical(lax.shift_right_logical(bits, half), half)
    tile = (slice(0, rows), slice(col0, col0 + lanes))
    w_ref[tile] = w_ref[tile] + pltpu.bitcast(zero, F32).astype(BF16)


def _state_conv(state_ref, w_ref, new):
    k1 = state_ref.shape[0]
    acc = w_ref[k1:k1 + 1, :] * new
    for k in range(k1):
        acc = acc + w_ref[k:k + 1, :] * state_ref[k]
    return acc


def _state_shift(out_ref, state_ref, new):
    k1 = state_ref.shape[0]
    out_ref[0:k1 - 1] = state_ref[1:k1]
    out_ref[k1 - 1] = new


def _mixer_sample_rows(x_ref, sa_ref, sc_ref, pre_ref, post_ref, win_ref, wout_ref, acw_ref,
                       lng_ref, lnb_ref, w00_ref, b0_ref, ccw_ref, ccb_ref, clg_ref, clb_ref,
                       o_ref, na_ref, nc_ref, v_ref, mix_ref):
    n = x_ref.shape[0]
    aw = acw_ref.shape[1]
    bw = w00_ref.shape[1]
    hd = bw // B_HEADS
    cw = ccw_ref.shape[1]
    rows = slice(0, n)
    x = x_ref[...]
    h = _rms(x, pre_ref[...]).astype(BF16)

    za = _dot(h, win_ref[:, 0:3 * aw])
    fa_new = za[:, aw:2 * aw] * za[:, 0:aw]
    conv_a = _state_conv(sa_ref, acw_ref, fa_new)
    mix_ref[rows, 0:aw] = (za[:, 2 * aw:3 * aw] * conv_a).astype(BF16)
    _state_shift(na_ref, sa_ref, fa_new)

    o_b = 3 * aw
    u = _gelu(_dot(h, win_ref[:, o_b:o_b + bw]))
    gv = _gelu(_dot(h, win_ref[:, o_b + bw:o_b + 2 * bw]))
    for hh in range(B_HEADS):
        cols = slice(hh * hd, (hh + 1) * hd)
        v = _ln(gv[:, cols], lng_ref[hh:hh + 1, :], lnb_ref[hh:hh + 1, :])
        v_ref[:, cols] = v
        sp = w00_ref[:, cols] * v + b0_ref[:, cols]
        mix_ref[rows, aw + hh * hd:aw + (hh + 1) * hd] = (u[:, cols] * sp).astype(BF16)

    o_c = o_b + 2 * bw
    zc = _dot(h, win_ref[:, o_c:o_c + 2 * cw])
    fc_new = zc[:, 0:cw] * jax.nn.sigmoid(zc[:, cw:2 * cw])
    acc = _state_conv(sc_ref, ccw_ref, fc_new) + ccb_ref[...]
    y_c = jax.nn.silu(_ln(acc, clg_ref[...], clb_ref[...]))
    mix_ref[rows, aw + bw:aw + bw + cw] = y_c.astype(BF16)
    _state_shift(nc_ref, sc_ref, fc_new)

    mix = _dot(mix_ref[rows, :], wout_ref[...])
    o_ref[...] = x + _rms(mix, post_ref[...])


def _mixer_body(x_ref, xs_ref, sa_ref, sc_ref, pre_ref, post_ref, win32_ref, wout32_ref,
                acw_ref, lng_ref, lnb_ref, ws_ref, bias_ref, w00_ref, b0_ref,
                ccw_ref, ccb_ref, clg_ref, clb_ref,
                o_ref, nap_ref, ncp_ref, os_ref, nas_ref, ncs_ref, v_ref,
                fa_ref, fc_ref, mix_ref, win_ref, wout_ref, *, n_prompt, tiles_per_seq, layer):
    s = pl.program_id(0)
    pre_ref, post_ref, ccb_ref, clg_ref, clb_ref = (
        _RowOf(r, layer) for r in (pre_ref, post_ref, ccb_ref, clg_ref, clb_ref))

    @pl.when(s == 0)
    def _():
        win_ref[...] = win32_ref[...].astype(BF16)
        wout_ref[...] = wout32_ref[...].astype(BF16)

    @pl.when(s < n_prompt)
    def _():
        _mixer_prompt_rows(lax.rem(s, tiles_per_seq), x_ref, pre_ref, post_ref, win_ref, wout_ref,
                           acw_ref, lng_ref, lnb_ref, ws_ref, bias_ref, ccw_ref, ccb_ref,
                           clg_ref, clb_ref, o_ref, nap_ref, ncp_ref, fa_ref, fc_ref, mix_ref)

    @pl.when(s == n_prompt)
    def _():
        _mixer_sample_rows(xs_ref, sa_ref, sc_ref, pre_ref, post_ref, win_ref, wout_ref, acw_ref,
                           lng_ref, lnb_ref, w00_ref, b0_ref, ccw_ref, ccb_ref, clg_ref, clb_ref,
                           os_ref, nas_ref, ncs_ref, v_ref, mix_ref)


def _mixer(x, xs, sa, sc, pre, post, w_in, w_out, a_cw, lng, lnb, ws, bias_t, w00, b0,
           c_cw, c_cb, clg, clb, layer):
    b, s, d = x.shape
    ns = xs.shape[0]
    d_in = _shape(w_in)[1]
    d_mix = _shape(w_out)[0]
    aw, cw = _shape(a_cw)[1], _shape(c_cw)[1]
    bw = _shape(w00)[1]
    assert _shape(ws)[1] == CHUNK and _shape(ws)[2] == CHUNK and CHUNK % CONV_ROWS == 0
    assert aw % V7X_LANES == 0 and cw % V7X_LANES == 0
    args = [x, xs, sa, sc, pre, post, w_in, w_out, a_cw, lng, lnb, ws, bias_t, w00, b0,
            c_cw, c_cb, clg, clb]
    sample_out = [(ns, d), _shape(sa), _shape(sc), (ns, bw)]
    fixed = (sum(_bytes(a) for a in args[1:]) + 2 * sum(_nbytes(o, F32) for o in sample_out)
             + _nbytes((d, d_in), BF16) + _nbytes((d_mix, d), BF16))
    per_row = 4 * _nbytes((d,), F32) + _nbytes((d_mix,), BF16) + _nbytes((aw + cw,), F32)
    tile = _pick_tile(s, fixed, per_row, multiple_of=CHUNK)
    assert ns <= tile
    blocks = fixed + tile * per_row
    tiles_per_seq = s // tile
    n_prompt = b * tiles_per_seq
    last = n_prompt - 1

    def seq_of(i):
        return lax.div(jnp.minimum(i, last), tiles_per_seq)

    def tile_of(i):
        return lax.rem(jnp.minimum(i, last), tiles_per_seq)

    xblk = pl.BlockSpec((None, tile, d), lambda i: (seq_of(i), tile_of(i), 0))
    return pl.pallas_call(
        functools.partial(_mixer_body, n_prompt=n_prompt, tiles_per_seq=tiles_per_seq,
                          layer=layer),
        grid=(n_prompt + 1,),
        in_specs=[xblk] + [_resident(a) for a in args[1:]],
        out_specs=[xblk,
                   pl.BlockSpec((None, A_KERNEL - 1, aw), lambda i: (seq_of(i), 0, 0)),
                   pl.BlockSpec((None, C_KERNEL - 1, cw), lambda i: (seq_of(i), 0, 0))]
                  + [_whole_out(o) for o in sample_out],
        out_shape=[jax.ShapeDtypeStruct((b, s, d), F32),
                   jax.ShapeDtypeStruct((b, A_KERNEL - 1, aw), F32),
                   jax.ShapeDtypeStruct((b, C_KERNEL - 1, cw), F32)]
                  + [jax.ShapeDtypeStruct(o, F32) for o in sample_out],
        scratch_shapes=[pltpu.VMEM((aw // V7X_LANES, A_HIST_PAD + tile, V7X_LANES), F32),
                        pltpu.VMEM((cw // V7X_LANES, C_HIST_PAD + tile, V7X_LANES), F32),
                        pltpu.VMEM((tile, d_mix), BF16),
                        pltpu.VMEM((d, d_in), BF16), pltpu.VMEM((d_mix, d), BF16)],
        compiler_params=pltpu.CompilerParams(
            dimension_semantics=("arbitrary",), vmem_limit_bytes=_vmem_limit(blocks)),
        name="mixer",
    )(*[_arr(a) for a in args])


def kernel(x_prompt, x_sample, state_conv_a, state_conv_c, p_prompt, p_sample, f1_pre, f1_post, f1_wg, f1_wu, f1_wd, m_pre, m_post, w_in, w_out, a_conv_w, b_ln_g, b_ln_b, b_ws, b_bias, c_conv_w, c_conv_b, c_ln_g, c_ln_b, f2_pre, f2_post, f2_wg, f2_wu, f2_wd, e_pre, e_post, e_wg, e_wp):
    depth = f1_wg.shape[0]
    nb, seq, d = x_prompt.shape
    ns = x_sample.shape[0]
    assert x_sample.shape[1] == 1
    hd = b_ws.shape[2]
    bw = B_HEADS * hd

    yp = x_prompt.reshape(nb * seq, d)
    ys = x_sample.reshape(ns, d)
    sa_t = jnp.transpose(state_conv_a, (0, 2, 1, 3))
    sc_t = jnp.transpose(state_conv_c, (0, 2, 1, 3))
    bias_t = jnp.transpose(b_bias, (0, 2, 1))
    w00 = jnp.repeat(b_ws[:, :, 0, 0], hd, axis=1).reshape(depth, 1, bw)
    b0 = jnp.repeat(b_bias[:, :, 0], hd, axis=1).reshape(depth, 1, bw)
    pp = p_prompt.reshape(depth, nb * seq, -1)
    ps = p_sample.reshape(depth, ns, -1)
    a_p, c_p, a_s, c_s, v_s = [], [], [], [], []
    for i in range(depth):
        row = lambda g: _Row(g, i)
        w = lambda a: _Layer(a, i)
        f1 = (row(f1_pre), row(f1_post), w(f1_wg), w(f1_wu), w(f1_wd))
        f2 = (row(f2_pre), row(f2_post), w(f2_wg), w(f2_wu), w(f2_wd))
        em = (row(e_pre), row(e_post), w(e_wg), w(e_wp))

        yp, ys = _ffn(yp, ys, f1, i)
        yp, na_p, nc_p, ys, na_s, nc_s, vr_s = _mixer(
            yp.reshape(nb, seq, d), ys, w(sa_t), w(sc_t),
            row(m_pre), row(m_post), w(w_in), w(w_out),
            w(a_conv_w), w(b_ln_g), w(b_ln_b), w(b_ws), w(bias_t), w(w00), w(b0),
            w(c_conv_w), row(c_conv_b), row(c_ln_g), row(c_ln_b), layer=i)
        yp, ys = _ffn(yp.reshape(nb * seq, d), ys, f2, i,
                      p=_Layer(pp, i), ps=_Layer(ps, i), embed_w=em)

        a_p.append(na_p)
        c_p.append(nc_p)
        a_s.append(na_s)
        c_s.append(nc_s)
        v_s.append(vr_s.reshape(ns, 1, bw))
    return (yp.reshape(nb, seq, d), ys.reshape(ns, 1, d), jnp.stack(a_p), jnp.stack(c_p),
            jnp.transpose(jnp.stack(a_s), (0, 2, 1, 3)),
            jnp.transpose(jnp.stack(c_s), (0, 2, 1, 3)), jnp.stack(v_s))
```

```python
import functools
import math
from typing import NamedTuple

import jax
import jax.numpy as jnp
from jax import lax
from jax.experimental import pallas as pl
from jax.experimental.pallas import tpu as pltpu

EPS = 1e-6
CHUNK = 128
A_KERNEL = 3
C_KERNEL = 31
B_HEADS = 4

V7X_SUBLANES = 8
V7X_LANES = 128
V7X_MXU_COLS = 256
V7X_VMEM_BYTES = 64 * 1024 * 1024

TILE_CANDIDATES = (1024, 512, 256, 128)
VMEM_HEADROOM = 5 << 20
FFN_SUB_ROWS = 256
FFN_ROW_PARAMS = (0, 1, 5, 6)
MIXER_SUB_ROWS = 1024
FF_COLS = V7X_MXU_COLS
CONV_ROWS = 32
A_HIST_PAD = V7X_SUBLANES
C_HIST_PAD = 4 * V7X_SUBLANES

GELU_C = math.sqrt(2.0 / math.pi)
GELU_CUBIC = 0.044715

F32 = jnp.float32
BF16 = jnp.bfloat16


def _vmem_limit(block_bytes):
    return min(block_bytes + 2 * VMEM_HEADROOM, V7X_VMEM_BYTES - (2 << 20))


def _pick_tile(n_rows, fixed_bytes, row_bytes, multiple_of=1):
    for tile in TILE_CANDIDATES:
        fits = fixed_bytes + tile * row_bytes <= V7X_VMEM_BYTES - VMEM_HEADROOM
        if n_rows % tile == 0 and tile % multiple_of == 0 and fits:
            return tile
    raise ValueError("no prompt tile fits VMEM")


def _rms(x, g):
    return x * lax.rsqrt(jnp.mean(x * x, axis=-1, keepdims=True) + EPS) * g


def _ln(x, g, b):
    mu = jnp.mean(x, axis=-1, keepdims=True)
    xc = x - mu
    var = jnp.mean(xc * xc, axis=-1, keepdims=True)
    return xc * lax.rsqrt(var + EPS) * g + b


def _gelu(x):
    inner = x * (GELU_C + (GELU_C * GELU_CUBIC) * (x * x))
    hx = 0.5 * x
    return hx + hx * jnp.tanh(inner)


def _dot(a, b):
    return jnp.dot(a, b, preferred_element_type=F32)


class _Layer(NamedTuple):
    arr: jax.Array
    layer: int


class _Row(NamedTuple):
    arr: jax.Array
    layer: int


class _RowOf:
    def __init__(self, ref, layer):
        self.ref, self.layer = ref, layer

    def __getitem__(self, idx):
        assert idx is Ellipsis
        return self.ref[self.layer:self.layer + 1, :]


def _shape(a):
    if isinstance(a, _Layer):
        return a.arr.shape[1:]
    return a.arr.shape if isinstance(a, _Row) else a.shape


def _arr(a):
    return a.arr if isinstance(a, (_Layer, _Row)) else a


def _nbytes(shape, dtype):
    n = 1
    for s in shape:
        n *= s
    return n * jnp.dtype(dtype).itemsize


def _bytes(a):
    return _nbytes(_shape(a), _arr(a).dtype)


def _resident(a):
    shape = _shape(a)
    zeros = (0,) * len(shape)
    if isinstance(a, _Layer):
        return pl.BlockSpec((None,) + shape, lambda *_: (a.layer,) + zeros,
                            pipeline_mode=pl.Buffered(1))
    return pl.BlockSpec(shape, lambda *_: zeros, pipeline_mode=pl.Buffered(1))


def _whole_out(shape):
    zeros = (0,) * len(shape)
    return pl.BlockSpec(shape, lambda *_: zeros)


def _rows(a, tile, n_tiles):
    cols = _shape(a)[1]
    last = n_tiles - 1
    if isinstance(a, _Layer):
        return pl.BlockSpec((None, tile, cols), lambda i: (a.layer, jnp.minimum(i, last), 0))
    return pl.BlockSpec((tile, cols), lambda i: (jnp.minimum(i, last), 0))


def _slab_store(ref, row0, val):
    for j in range(ref.shape[0]):
        ref[j, row0:row0 + val.shape[0], :] = val[:, j * V7X_LANES:(j + 1) * V7X_LANES]


def _slab_load(ref, row0, rows):
    return jnp.concatenate([ref[j, row0:row0 + rows, :] for j in range(ref.shape[0])], axis=1)


def _causal_conv(ref, w_ref, row0, rows):
    out = []
    for j in range(ref.shape[0]):
        lanes = slice(j * V7X_LANES, (j + 1) * V7X_LANES)
        acc = w_ref[0:1, lanes] * ref[j, row0:row0 + rows, :]
        for k in range(1, w_ref.shape[0]):
            acc = acc + w_ref[k:k + 1, lanes] * ref[j, row0 + k:row0 + k + rows, :]
        out.append(acc)
    return jnp.concatenate(out, axis=1)


def _ffn_rows(x_ref, p_ref, w, o_ref, act_ref):
    pre_ref, post_ref, wg_ref, wu_ref, wd_ref = w[:5]
    d_ff = wg_ref.shape[1]
    n = x_ref.shape[0]
    sub = min(n, FFN_SUB_ROWS)
    subs = [slice(r0, r0 + sub) for r0 in range(0, n, sub)]
    if p_ref is not None:
        epre_ref, epost_ref, ewg_ref, ewp_ref = w[5:]
        pes = [_dot(p_ref[r, :].astype(BF16), ewp_ref[...]) for r in subs]
    xs = [x_ref[r, :] for r in subs]
    hs = [_rms(x, pre_ref[...]).astype(BF16) for x in xs]
    for r, h in zip(subs, hs):
        for c0 in range(0, d_ff, FF_COLS):
            g = _dot(h, wg_ref[:, c0:c0 + FF_COLS])
            u = _dot(h, wu_ref[:, c0:c0 + FF_COLS])
            act_ref[r, c0:c0 + FF_COLS] = (jax.nn.silu(g) * u).astype(BF16)
    ys = [_dot(act_ref[r, :], wd_ref[...]) for r in subs]
    xs = [x + 0.5 * _rms(y, post_ref[...]) for x, y in zip(xs, ys)]
    if p_ref is not None:
        hs = [_rms(x, epre_ref[...]).astype(BF16) for x in xs]
        gates = [jax.nn.sigmoid(_dot(h, ewg_ref[...])) for h in hs]
        xs = [x + _rms(gate * pe, epost_ref[...]) for x, gate, pe in zip(xs, gates, pes)]
    for r, x in zip(subs, xs):
        o_ref[r, :] = x


def _ffn_body(*refs, with_embed, n_prompt, layer):
    if with_embed:
        x_ref, xs_ref, p_ref, ps_ref = refs[:4]
        w = list(refs[4:-3])
    else:
        x_ref, xs_ref = refs[:2]
        p_ref = ps_ref = None
        w = list(refs[2:-3])
    for k in FFN_ROW_PARAMS:
        if k < len(w):
            w[k] = _RowOf(w[k], layer)
    o_ref, os_ref, act_ref = refs[-3:]
    i = pl.program_id(0)

    @pl.when(i < n_prompt)
    def _():
        _ffn_rows(x_ref, p_ref, w, o_ref, act_ref)

    @pl.when(i == n_prompt)
    def _():
        _ffn_rows(xs_ref, ps_ref, w, os_ref, act_ref)


def _ffn(x, xs, ffn_w, layer, p=None, ps=None, embed_w=None):
    n, d = x.shape
    ns = xs.shape[0]
    d_ff = _shape(ffn_w[2])[1]
    with_embed = embed_w is not None
    weights = list(ffn_w) + (list(embed_w) if with_embed else [])
    fixed = sum(_bytes(a) for a in weights) + 3 * _bytes(xs)
    per_row = 5 * _nbytes((d,), F32) + _nbytes((d_ff,), BF16)
    if with_embed:
        fixed += _bytes(ps)
        per_row += 2 * _nbytes((_shape(p)[1],), F32)
    tile = _pick_tile(n, fixed, per_row)
    assert ns <= tile and d_ff % FF_COLS == 0
    blocks = fixed + tile * per_row
    n_prompt = n // tile
    row = _rows(x, tile, n_prompt)
    args = [x, xs]
    in_specs = [row, _resident(xs)]
    if with_embed:
        args += [p, ps]
        in_specs += [_rows(p, tile, n_prompt), _resident(ps)]
    return pl.pallas_call(
        functools.partial(_ffn_body, with_embed=with_embed, n_prompt=n_prompt, layer=layer),
        grid=(n_prompt + 1,),
        in_specs=in_specs + [_resident(a) for a in weights],
        out_specs=[row, _whole_out((ns, d))],
        out_shape=[jax.ShapeDtypeStruct((n, d), F32), jax.ShapeDtypeStruct((ns, d), F32)],
        scratch_shapes=[pltpu.VMEM((tile, d_ff), BF16)],
        compiler_params=pltpu.CompilerParams(
            dimension_semantics=("parallel",), vmem_limit_bytes=_vmem_limit(blocks)),
        name="ffn_embed" if with_embed else "ffn",
    )(*[_arr(a) for a in args + weights])


def _mixer_prompt_rows(t, x_ref, pre_ref, post_ref, win_ref, wout_ref, acw_ref, lng_ref, lnb_ref,
                       ws_ref, bias_ref, ccw_ref, ccb_ref, clg_ref, clb_ref,
                       o_ref, na_ref, nc_ref, fa_ref, fc_ref, mix_ref):
    tt = x_ref.shape[0]
    aw = acw_ref.shape[1]
    bw = ws_ref.shape[0] * ws_ref.shape[1]
    hd = ws_ref.shape[1]
    cw = ccw_ref.shape[1]

    @pl.when(t == 0)
    def _():
        fa_ref[:, 0:A_HIST_PAD, :] = jnp.zeros((fa_ref.shape[0], A_HIST_PAD, V7X_LANES), F32)
        fc_ref[:, 0:C_HIST_PAD, :] = jnp.zeros((fc_ref.shape[0], C_HIST_PAD, V7X_LANES), F32)

    sub = min(tt, MIXER_SUB_ROWS)
    subs = [slice(r0, r0 + sub) for r0 in range(0, tt, sub)]
    hs = [_rms(x_ref[r, :], pre_ref[...]).astype(BF16) for r in subs]
    o_b = 3 * aw
    o_c = o_b + 2 * bw

    for r, h in zip(subs, hs):
        zc = _dot(h, win_ref[:, o_c:o_c + 2 * cw])
        _slab_store(fc_ref, C_HIST_PAD + r.start, zc[:, 0:cw] * jax.nn.sigmoid(zc[:, cw:2 * cw]))
    base = C_HIST_PAD - (C_KERNEL - 1)
    for r0 in range(0, tt, CONV_ROWS):
        acc = _causal_conv(fc_ref, ccw_ref, base + r0, CONV_ROWS) + ccb_ref[...]
        y_c = jax.nn.silu(_ln(acc, clg_ref[...], clb_ref[...]))
        mix_ref[r0:r0 + CONV_ROWS, aw + bw:aw + bw + cw] = y_c.astype(BF16)
    last_c = _slab_load(fc_ref, base + tt, C_KERNEL - 1)
    nc_ref[...] = last_c
    _slab_store(fc_ref, base, last_c)

    base = A_HIST_PAD - (A_KERNEL - 1)
    for r, h in zip(subs, hs):
        za = _dot(h, win_ref[:, 0:3 * aw])
        _slab_store(fa_ref, A_HIST_PAD + r.start, za[:, aw:2 * aw] * za[:, 0:aw])
        for q0 in range(0, sub, CONV_ROWS):
            r0 = r.start + q0
            y_a = za[q0:q0 + CONV_ROWS, 2 * aw:3 * aw] * _causal_conv(fa_ref, acw_ref, base + r0, CONV_ROWS)
            mix_ref[r0:r0 + CONV_ROWS, 0:aw] = y_a.astype(BF16)
    last_a = _slab_load(fa_ref, base + tt, A_KERNEL - 1)
    na_ref[...] = last_a
    _slab_store(fa_ref, base, last_a)

    row_i = lax.broadcasted_iota(jnp.int32, (CHUNK, CHUNK), 0)
    col_j = lax.broadcasted_iota(jnp.int32, (CHUNK, CHUNK), 1)
    w_trils = [jnp.where(col_j <= row_i, ws_ref[hh], 0.0).astype(BF16) for hh in range(B_HEADS)]
    n_chunks = sub // CHUNK
    us = [_gelu(_dot(h, win_ref[:, o_b:o_b + bw])) for h in hs]
    gvs = [_gelu(_dot(h, win_ref[:, o_b + bw:o_b + 2 * bw])) for h in hs]
    for r, u, gv in zip(subs, us, gvs):
        for hh in range(B_HEADS):
            cols = slice(hh * hd, (hh + 1) * hd)
            v = _ln(gv[:, cols], lng_ref[hh:hh + 1, :], lnb_ref[hh:hh + 1, :]).astype(BF16)
            bias = bias_ref[:, hh:hh + 1]
            v_cat = jnp.concatenate(
                [v[c * CHUNK:(c + 1) * CHUNK, :] for c in range(n_chunks)], axis=1)
            sp = _dot(w_trils[hh], v_cat)
            for c in range(n_chunks):
                y_b = u[c * CHUNK:(c + 1) * CHUNK, cols] * (sp[:, c * hd:(c + 1) * hd] + bias)
                rows = slice(r.start + c * CHUNK, r.start + (c + 1) * CHUNK)
                mix_ref[rows, aw + hh * hd:aw + (hh + 1) * hd] = y_b.astype(BF16)

    mixes = [_dot(mix_ref[r, :], wout_ref[...]) for r in subs]
    for r, mix in zip(subs, mixes):
        o_ref[r, :] = x_ref[r, :] + _rms(mix, post_ref[...])


def _state_conv(state_ref, w_ref, new):
    k1 = state_ref.shape[0]
    acc = w_ref[k1:k1 + 1, :] * new
    for k in range(k1):
        acc = acc + w_ref[k:k + 1, :] * state_ref[k]
    return acc


def _state_shift(out_ref, state_ref, new):
    k1 = state_ref.shape[0]
    out_ref[0:k1 - 1] = state_ref[1:k1]
    out_ref[k1 - 1] = new


def _mixer_sample_rows(x_ref, sa_ref, sc_ref, pre_ref, post_ref, win_ref, wout_ref, acw_ref,
                       lng_ref, lnb_ref, w00_ref, b0_ref, ccw_ref, ccb_ref, clg_ref, clb_ref,
                       o_ref, na_ref, nc_ref, v_ref, mix_ref):
    n = x_ref.shape[0]
    aw = acw_ref.shape[1]
    bw = w00_ref.shape[1]
    hd = bw // B_HEADS
    cw = ccw_ref.shape[1]
    rows = slice(0, n)
    x = x_ref[...]
    h = _rms(x, pre_ref[...]).astype(BF16)

    za = _dot(h, win_ref[:, 0:3 * aw])
    fa_new = za[:, aw:2 * aw] * za[:, 0:aw]
    conv_a = _state_conv(sa_ref, acw_ref, fa_new)
    mix_ref[rows, 0:aw] = (za[:, 2 * aw:3 * aw] * conv_a).astype(BF16)
    _state_shift(na_ref, sa_ref, fa_new)

    o_b = 3 * aw
    u = _gelu(_dot(h, win_ref[:, o_b:o_b + bw]))
    gv = _gelu(_dot(h, win_ref[:, o_b + bw:o_b + 2 * bw]))
    for hh in range(B_HEADS):
        cols = slice(hh * hd, (hh + 1) * hd)
        v = _ln(gv[:, cols], lng_ref[hh:hh + 1, :], lnb_ref[hh:hh + 1, :])
        v_ref[:, cols] = v
        sp = w00_ref[:, cols] * v + b0_ref[:, cols]
        mix_ref[rows, aw + hh * hd:aw + (hh + 1) * hd] = (u[:, cols] * sp).astype(BF16)

    o_c = o_b + 2 * bw
    zc = _dot(h, win_ref[:, o_c:o_c + 2 * cw])
    fc_new = zc[:, 0:cw] * jax.nn.sigmoid(zc[:, cw:2 * cw])
    acc = _state_conv(sc_ref, ccw_ref, fc_new) + ccb_ref[...]
    y_c = jax.nn.silu(_ln(acc, clg_ref[...], clb_ref[...]))
    mix_ref[rows, aw + bw:aw + bw + cw] = y_c.astype(BF16)
    _state_shift(nc_ref, sc_ref, fc_new)

    mix = _dot(mix_ref[rows, :], wout_ref[...])
    o_ref[...] = x + _rms(mix, post_ref[...])


def _mixer_body(x_ref, xs_ref, sa_ref, sc_ref, pre_ref, post_ref, win32_ref, wout32_ref,
                acw_ref, lng_ref, lnb_ref, ws_ref, bias_ref, w00_ref, b0_ref,
                ccw_ref, ccb_ref, clg_ref, clb_ref,
                o_ref, nap_ref, ncp_ref, os_ref, nas_ref, ncs_ref, v_ref,
                fa_ref, fc_ref, mix_ref, win_ref, wout_ref, *, n_prompt, tiles_per_seq, layer):
    s = pl.program_id(0)
    pre_ref, post_ref, ccb_ref, clg_ref, clb_ref = (
        _RowOf(r, layer) for r in (pre_ref, post_ref, ccb_ref, clg_ref, clb_ref))

    @pl.when(s == 0)
    def _():
        win_ref[...] = win32_ref[...].astype(BF16)
        wout_ref[...] = wout32_ref[...].astype(BF16)

    @pl.when(s < n_prompt)
    def _():
        _mixer_prompt_rows(lax.rem(s, tiles_per_seq), x_ref, pre_ref, post_ref, win_ref, wout_ref,
                           acw_ref, lng_ref, lnb_ref, ws_ref, bias_ref, ccw_ref, ccb_ref,
                           clg_ref, clb_ref, o_ref, nap_ref, ncp_ref, fa_ref, fc_ref, mix_ref)

    @pl.when(s == n_prompt)
    def _():
        _mixer_sample_rows(xs_ref, sa_ref, sc_ref, pre_ref, post_ref, win_ref, wout_ref, acw_ref,
                           lng_ref, lnb_ref, w00_ref, b0_ref, ccw_ref, ccb_ref, clg_ref, clb_ref,
                           os_ref, nas_ref, ncs_ref, v_ref, mix_ref)


def _mixer(x, xs, sa, sc, pre, post, w_in, w_out, a_cw, lng, lnb, ws, bias_t, w00, b0,
           c_cw, c_cb, clg, clb, layer):
    b, s, d = x.shape
    ns = xs.shape[0]
    d_in = _shape(w_in)[1]
    d_mix = _shape(w_out)[0]
    aw, cw = _shape(a_cw)[1], _shape(c_cw)[1]
    bw = _shape(w00)[1]
    assert _shape(ws)[1] == CHUNK and _shape(ws)[2] == CHUNK and CHUNK % CONV_ROWS == 0
    assert aw % V7X_LANES == 0 and cw % V7X_LANES == 0
    args = [x, xs, sa, sc, pre, post, w_in, w_out, a_cw, lng, lnb, ws, bias_t, w00, b0,
            c_cw, c_cb, clg, clb]
    sample_out = [(ns, d), _shape(sa), _shape(sc), (ns, bw)]
    fixed = (sum(_bytes(a) for a in args[1:]) + 2 * sum(_nbytes(o, F32) for o in sample_out)
             + _nbytes((d, d_in), BF16) + _nbytes((d_mix, d), BF16))
    per_row = 4 * _nbytes((d,), F32) + _nbytes((d_mix,), BF16) + _nbytes((aw + cw,), F32)
    tile = _pick_tile(s, fixed, per_row, multiple_of=CHUNK)
    assert ns <= tile
    blocks = fixed + tile * per_row
    tiles_per_seq = s // tile
    n_prompt = b * tiles_per_seq
    last = n_prompt - 1

    def seq_of(i):
        return lax.div(jnp.minimum(i, last), tiles_per_seq)

    def tile_of(i):
        return lax.rem(jnp.minimum(i, last), tiles_per_seq)

    xblk = pl.BlockSpec((None, tile, d), lambda i: (seq_of(i), tile_of(i), 0))
    return pl.pallas_call(
        functools.partial(_mixer_body, n_prompt=n_prompt, tiles_per_seq=tiles_per_seq,
                          layer=layer),
        grid=(n_prompt + 1,),
        in_specs=[xblk] + [_resident(a) for a in args[1:]],
        out_specs=[xblk,
                   pl.BlockSpec((None, A_KERNEL - 1, aw), lambda i: (seq_of(i), 0, 0)),
                   pl.BlockSpec((None, C_KERNEL - 1, cw), lambda i: (seq_of(i), 0, 0))]
                  + [_whole_out(o) for o in sample_out],
        out_shape=[jax.ShapeDtypeStruct((b, s, d), F32),
                   jax.ShapeDtypeStruct((b, A_KERNEL - 1, aw), F32),
                   jax.ShapeDtypeStruct((b, C_KERNEL - 1, cw), F32)]
                  + [jax.ShapeDtypeStruct(o, F32) for o in sample_out],
        scratch_shapes=[pltpu.VMEM((aw // V7X_LANES, A_HIST_PAD + tile, V7X_LANES), F32),
                        pltpu.VMEM((cw // V7X_LANES, C_HIST_PAD + tile, V7X_LANES), F32),
                        pltpu.VMEM((tile, d_mix), BF16),
                        pltpu.VMEM((d, d_in), BF16), pltpu.VMEM((d_mix, d), BF16)],
        compiler_params=pltpu.CompilerParams(
            dimension_semantics=("arbitrary",), vmem_limit_bytes=_vmem_limit(blocks)),
        name="mixer",
    )(*[_arr(a) for a in args])


def kernel(x_prompt, x_sample, state_conv_a, state_conv_c, p_prompt, p_sample, f1_pre, f1_post, f1_wg, f1_wu, f1_wd, m_pre, m_post, w_in, w_out, a_conv_w, b_ln_g, b_ln_b, b_ws, b_bias, c_conv_w, c_conv_b, c_ln_g, c_ln_b, f2_pre, f2_post, f2_wg, f2_wu, f2_wd, e_pre, e_post, e_wg, e_wp):
    depth = f1_wg.shape[0]
    nb, seq, d = x_prompt.shape
    ns = x_sample.shape[0]
    assert x_sample.shape[1] == 1
    hd = b_ws.shape[2]
    bw = B_HEADS * hd

    yp = x_prompt.reshape(nb * seq, d)
    ys = x_sample.reshape(ns, d)
    sa_t = jnp.transpose(state_conv_a, (0, 2, 1, 3))
    sc_t = jnp.transpose(state_conv_c, (0, 2, 1, 3))
    bias_t = jnp.transpose(b_bias, (0, 2, 1))
    w00 = jnp.repeat(b_ws[:, :, 0, 0], hd, axis=1).reshape(depth, 1, bw)
    b0 = jnp.repeat(b_bias[:, :, 0], hd, axis=1).reshape(depth, 1, bw)
    pp = p_prompt.reshape(depth, nb * seq, -1)
    ps = p_sample.reshape(depth, ns, -1)
    a_p, c_p, a_s, c_s, v_s = [], [], [], [], []
    for i in range(depth):
        row = lambda g: _Row(g, i)
        w = lambda a: _Layer(a, i)
        f1 = (row(f1_pre), row(f1_post), w(f1_wg), w(f1_wu), w(f1_wd))
        f2 = (row(f2_pre), row(f2_post), w(f2_wg), w(f2_wu), w(f2_wd))
        em = (row(e_pre), row(e_post), w(e_wg), w(e_wp))

        yp, ys = _ffn(yp, ys, f1, i)
        yp, na_p, nc_p, ys, na_s, nc_s, vr_s = _mixer(
            yp.reshape(nb, seq, d), ys, w(sa_t), w(sc_t),
            row(m_pre), row(m_post), w(w_in), w(w_out),
            w(a_conv_w), w(b_ln_g), w(b_ln_b), w(b_ws), w(bias_t), w(w00), w(b0),
            w(c_conv_w), row(c_conv_b), row(c_ln_g), row(c_ln_b), layer=i)
        yp, ys = _ffn(yp.reshape(nb * seq, d), ys, f2, i,
                      p=_Layer(pp, i), ps=_Layer(ps, i), embed_w=em)

        a_p.append(na_p)
        c_p.append(nc_p)
        a_s.append(na_s)
        c_s.append(nc_s)
        v_s.append(vr_s.reshape(ns, 1, bw))
    return (yp.reshape(nb, seq, d), ys.reshape(ns, 1, d), jnp.stack(a_p), jnp.stack(c_p),
            jnp.transpose(jnp.stack(a_s), (0, 2, 1, 3)),
            jnp.transpose(jnp.stack(c_s), (0, 2, 1, 3)), jnp.stack(v_s))
```
